```python
import jax, jax.numpy as jnp
from jax import lax
import numpy as np

D_MODEL = 2048
BATCH = 16
SEQ = 2048
DEPTH = 4
DEC_BATCH = 32
DEC_SEQ = 16
PAST_LEN = 1024

CHUNK = 64
N_MIXERS = 2
N_FOX = (DEPTH + 1) // 2
N_HGRN = DEPTH // 2
FOX_HEADS = 16
FOX_HEAD_DIM = D_MODEL // FOX_HEADS
FOX_Q_BLOCK = 128
HGRN_HEADS = 16
HGRN_KDIM = D_MODEL // HGRN_HEADS
HGRN_VDIM = D_MODEL // HGRN_HEADS
HGRN_BLOCK = 16
D_FF = ((8 * D_MODEL // 3 + 127) // 128) * 128
DEEPNORM_ALPHA = (2 * DEPTH) ** 0.25
DEEPNORM_BETA = (8 * DEPTH) ** -0.25
LN_EPS = 1e-5
RMS_EPS = 1e-6
NEG_INF = -1e30

kernel_name = 'fox_hgrn2_macaron_deepnorm_stream_step'

F32 = jnp.float32


def layer_norm(x, g, b):
    xf = x.astype(F32)
    mu = jnp.mean(xf, axis=-1, keepdims=True)
    var = jnp.mean(jnp.square(xf - mu), axis=-1, keepdims=True)
    return ((xf - mu) * lax.rsqrt(var + LN_EPS) * g.astype(F32) + b.astype(F32)).astype(x.dtype)


def swiglu_ffn(x, w_up, w_down):
    gate, up = jnp.split(x @ w_up, 2, axis=-1)
    return (jax.nn.silu(gate) * up) @ w_down


def fox_block(q_blk, q_pos, cq_blk, k, v, k_pos, ck_t):
    s = jnp.einsum('bqhd,bkhd->bhqk', q_blk, k) * (FOX_HEAD_DIM ** -0.5)
    s = s + jnp.transpose(cq_blk, (0, 2, 1))[..., None] - ck_t[:, :, None, :]
    mask = k_pos[None, :] <= q_pos[:, None]
    s = jnp.where(mask, s, NEG_INF)
    p = jax.nn.softmax(s, axis=-1)
    return jnp.einsum('bhqk,bkhd->bqhd', p, v)


def fox_mixer(x, w_in, b_f, w_out, past):
    B, T, _ = x.shape
    q, k, v, f_logit = jnp.split(x @ w_in, [D_MODEL, 2 * D_MODEL, 3 * D_MODEL], axis=-1)
    hd = (B, T, FOX_HEADS, FOX_HEAD_DIM)
    q, k, v = q.reshape(hd), k.reshape(hd), v.reshape(hd)
    logf = jax.nn.log_sigmoid((f_logit + b_f).astype(F32))
    if past is None:
        k_all, v_all, logf_all, offset = k, v, logf, 0
    else:
        k_past, v_past, logf_past = past
        k_all = jnp.concatenate([k_past.astype(k.dtype), k], axis=1)
        v_all = jnp.concatenate([v_past.astype(v.dtype), v], axis=1)
        logf_all = jnp.concatenate([logf_past.astype(F32), logf], axis=1)
        offset = k_past.shape[1]
    c = jnp.cumsum(logf_all, axis=1)
    ck_t = jnp.transpose(c, (0, 2, 1))
    cq = c[:, offset:]
    kf, vf, qf = k_all.astype(F32), v_all.astype(F32), q.astype(F32)
    k_pos = jnp.arange(k_all.shape[1])
    q_pos = offset + jnp.arange(T)
    if T % FOX_Q_BLOCK == 0:
        nb = T // FOX_Q_BLOCK
        qb = qf.reshape(B, nb, FOX_Q_BLOCK, FOX_HEADS, FOX_HEAD_DIM).swapaxes(0, 1)
        cqb = cq.reshape(B, nb, FOX_Q_BLOCK, FOX_HEADS).swapaxes(0, 1)
        pb = q_pos.reshape(nb, FOX_Q_BLOCK)
        o = lax.map(lambda a: fox_block(a[0], a[1], a[2], kf, vf, k_pos, ck_t), (qb, pb, cqb))
        o = o.swapaxes(0, 1).reshape(B, T, D_MODEL)
    else:
        o = fox_block(qf, q_pos, cq, kf, vf, k_pos, ck_t).reshape(B, T, D_MODEL)
    y = o.astype(x.dtype) @ w_out
    return y, (k, v, logf.astype(x.dtype))


def hgrn2_recurrence(q, k, v, g, s0):
    B, T, H, K = q.shape
    V = v.shape[-1]
    n = T // HGRN_BLOCK
    def blocks(a):
        return a.reshape(B, n, HGRN_BLOCK, *a.shape[2:]).swapaxes(0, 1)
    mask = jnp.tril(jnp.ones((HGRN_BLOCK, HGRN_BLOCK), dtype=bool))
    mid = HGRN_BLOCK // 2
    def step(S, inp):
        qb, kb, vb, gb = inp
        b = jnp.cumsum(gb, axis=1)
        b_mid = b[:, mid:mid + 1]
        qe = qb * jnp.exp(b - b_mid)
        ke = kb * jnp.exp(b_mid - b)
        A = jnp.einsum('bthk,bshk->bhts', qe, ke)
        A = jnp.where(mask, A, 0.0)
        o = jnp.einsum('bhts,bshv->bthv', A, vb) + jnp.einsum('bthk,bhkv->bthv', qb * jnp.exp(b), S)
        b_last = b[:, -1]
        S = S * jnp.exp(b_last)[..., None] + jnp.einsum('bshk,bshv->bhkv', kb * jnp.exp(b_last[:, None] - b), vb)
        return S, o
    S, o = lax.scan(step, s0, (blocks(q), blocks(k), blocks(v), blocks(g)))
    return o.swapaxes(0, 1).reshape(B, T, H, V), S


def hgrn2_mixer(x, w_in, lb, norm_g, w_out, s0):
    B, T, _ = x.shape
    q, z, i_in, g_out = jnp.split(x @ w_in, 4, axis=-1)
    shp = (B, T, HGRN_HEADS, HGRN_KDIM)
    vshp = (B, T, HGRN_HEADS, HGRN_VDIM)
    lbf = lb.astype(F32).reshape(HGRN_HEADS, HGRN_KDIM)
    zf = z.astype(F32).reshape(shp)
    logf = jnp.logaddexp(jnp.log(lbf), jnp.log1p(-lbf) + jax.nn.log_sigmoid(zf))
    kk = (1.0 - lbf) * jax.nn.sigmoid(-zf)
    qf = jax.nn.silu(q.astype(F32)).reshape(shp) * (HGRN_KDIM ** -0.5)
    vf = i_in.astype(F32).reshape(vshp)
    pad = (-T) % HGRN_BLOCK
    if pad:
        pw = ((0, 0), (0, pad), (0, 0), (0, 0))
        qf, kk, vf, logf = jnp.pad(qf, pw), jnp.pad(kk, pw), jnp.pad(vf, pw), jnp.pad(logf, pw)
    o, S = hgrn2_recurrence(qf, kk, vf, logf, s0.astype(F32))
    o = o[:, :T]
    o = o * lax.rsqrt(jnp.mean(jnp.square(o), axis=-1, keepdims=True) + RMS_EPS) * norm_g.astype(F32)
    o = o * jax.nn.silu(g_out.astype(F32).reshape(vshp))
    y = o.reshape(B, T, D_MODEL).astype(x.dtype) @ w_out
    return y, S.astype(x.dtype)


def setup_inputs(seed: int = 0) -> dict:
    key = jax.random.key(seed)
    ks = jax.random.split(key, 20)
    def nrm(k, shape, scale):
        return jax.random.normal(k, shape, F32) * scale
    d = D_MODEL
    return {
        'x_prompt': nrm(ks[0], (BATCH, SEQ, d), 1.0),
        'x_sample': nrm(ks[1], (DEC_BATCH, DEC_SEQ, d), 1.0),
        'cache_fox_k': nrm(ks[2], (N_FOX, DEC_BATCH, PAST_LEN, FOX_HEADS, FOX_HEAD_DIM), 1.0),
        'cache_fox_v': nrm(ks[3], (N_FOX, DEC_BATCH, PAST_LEN, FOX_HEADS, FOX_HEAD_DIM), 1.0),
        'cache_fox_logf': jax.nn.log_sigmoid(nrm(ks[4], (N_FOX, DEC_BATCH, PAST_LEN, FOX_HEADS), 1.0) + 2.0),
        'state_hgrn': nrm(ks[5], (N_HGRN, DEC_BATCH, HGRN_HEADS, HGRN_KDIM, HGRN_VDIM), 0.5),
        'ln_g': 1.0 + nrm(ks[6], (DEPTH, 3, d), 0.02),
        'ln_b': nrm(ks[7], (DEPTH, 3, d), 0.02),
        'ffn1_up': nrm(ks[8], (DEPTH, d, 2 * D_FF), d ** -0.5 * DEEPNORM_BETA),
        'ffn1_down': nrm(ks[9], (DEPTH, D_FF, d), D_FF ** -0.5 * DEEPNORM_BETA),
        'ffn2_up': nrm(ks[10], (DEPTH, d, 2 * D_FF), d ** -0.5 * DEEPNORM_BETA),
        'ffn2_down': nrm(ks[11], (DEPTH, D_FF, d), D_FF ** -0.5 * DEEPNORM_BETA),
        'fox_w_in': nrm(ks[12], (N_FOX, d, 3 * d + FOX_HEADS), d ** -0.5),
        'fox_b_f': nrm(ks[13], (N_FOX, FOX_HEADS), 0.1),
        'fox_w_out': nrm(ks[14], (N_FOX, d, d), d ** -0.5 * DEEPNORM_BETA),
        'hgrn_w_in': nrm(ks[15], (N_HGRN, d, 4 * d), d ** -0.5),
        'hgrn_lb': 1.0 + nrm(ks[16], (DEPTH, d), 0.02),
        'hgrn_norm_g': 1.0 + nrm(ks[17], (N_HGRN, HGRN_VDIM), 0.02),
        'hgrn_w_out': nrm(ks[18], (N_HGRN, d, d), d ** -0.5 * DEEPNORM_BETA),
    }


def reference(x_prompt, x_sample, cache_fox_k, cache_fox_v, cache_fox_logf, state_hgrn,
              ln_g, ln_b, ffn1_up, ffn1_down, ffn2_up, ffn2_down,
              fox_w_in, fox_b_f, fox_w_out, hgrn_w_in, hgrn_lb, hgrn_norm_g, hgrn_w_out):
    lb_soft = jax.nn.softmax(hgrn_lb.astype(F32), axis=0)
    lb_all = jnp.cumsum(lb_soft, axis=0) - lb_soft[0]

    def run_layer(x, i, fox_past, hgrn_s0):
        j = i // N_MIXERS
        x = layer_norm(DEEPNORM_ALPHA * x + 0.5 * swiglu_ffn(x, ffn1_up[i], ffn1_down[i]), ln_g[i, 0], ln_b[i, 0])
        if i % N_MIXERS == 0:
            m, st = fox_mixer(x, fox_w_in[j], fox_b_f[j], fox_w_out[j], fox_past)
        else:
            m, st = hgrn2_mixer(x, hgrn_w_in[j], lb_all[i], hgrn_norm_g[j], hgrn_w_out[j], hgrn_s0)
        x = layer_norm(DEEPNORM_ALPHA * x + m, ln_g[i, 1], ln_b[i, 1])
        x = layer_norm(DEEPNORM_ALPHA * x + 0.5 * swiglu_ffn(x, ffn2_up[i], ffn2_down[i]), ln_g[i, 2], ln_b[i, 2])
        return x, st

    yp, ys = x_prompt, x_sample
    fox_p, fox_s, hg_p, hg_s = [], [], [], []
    for i in range(DEPTH):
        j = i // N_MIXERS
        if i % N_MIXERS == 0:
            yp, st_p = run_layer(yp, i, None, None)
            ys, st_s = run_layer(ys, i, (cache_fox_k[j], cache_fox_v[j], cache_fox_logf[j]), None)
            fox_p.append(st_p)
            fox_s.append(st_s)
        else:
            s0 = jnp.zeros((x_prompt.shape[0], HGRN_HEADS, HGRN_KDIM, HGRN_VDIM), F32)
            yp, st_p = run_layer(yp, i, None, s0)
            ys, st_s = run_layer(ys, i, None, state_hgrn[j])
            hg_p.append(st_p)
            hg_s.append(st_s)

    fox_k_prompt = jnp.stack([s[0] for s in fox_p])
    fox_v_prompt = jnp.stack([s[1] for s in fox_p])
    fox_logf_prompt = jnp.stack([s[2] for s in fox_p])
    hgrn_state_prompt = jnp.stack(hg_p)
    fox_k_sample = jnp.stack([s[0] for s in fox_s])
    fox_v_sample = jnp.stack([s[1] for s in fox_s])
    fox_logf_sample = jnp.stack([s[2] for s in fox_s])
    hgrn_state_sample = jnp.stack(hg_s)
    return (yp, ys, fox_k_prompt, fox_v_prompt, fox_logf_prompt, hgrn_state_prompt,
            fox_k_sample, fox_v_sample, fox_logf_sample, hgrn_state_sample)
```

```python
import functools

import jax
import jax.numpy as jnp
from jax import lax
from jax.experimental import pallas as pl
from jax.experimental.pallas import tpu as pltpu

F32 = jnp.float32
BF16 = jnp.bfloat16

LN_EPS = 1e-5
RMS_EPS = 1e-6
NEG_INF = -1e30

LANES = 128
HGRN_SUB = 16
HGRN_CHUNK = 128
VMEM_LIMIT_BYTES = 48 * 1024 * 1024


def _round_up(n, m):
    return (n + m - 1) // m * m


def _pick_tile(n, candidates):
    for c in candidates:
        if n % c == 0:
            return c
    return n


def _cparams(*semantics):
    return pltpu.CompilerParams(dimension_semantics=semantics,
                                vmem_limit_bytes=VMEM_LIMIT_BYTES)


def _dot(a, b):
    return jnp.dot(a, b, preferred_element_type=F32)


def _dot_nt(a, b):
    return lax.dot_general(a, b, (((1,), (1,)), ((), ())), preferred_element_type=F32)


def _dot_tn(a, b):
    return lax.dot_general(a, b, (((0,), (0,)), ((), ())), preferred_element_type=F32)


def _sigmoid(x):
    return 1.0 / (1.0 + jnp.exp(-x))


def _log_sigmoid(x):
    return jnp.minimum(x, 0.0) - jnp.log1p(jnp.exp(-jnp.abs(x)))


def _layer_norm(r, g, b):
    mu = jnp.mean(r, axis=-1, keepdims=True)
    c = r - mu
    var = jnp.mean(c * c, axis=-1, keepdims=True)
    return c * lax.rsqrt(var + LN_EPS) * g + b


def _ffn_ln_kernel(x_ref, wg_ref, wu_ref, wd_ref, g_ref, b_ref, o_ref, xb_ref, acc_ref, *, alpha):
    f = pl.program_id(1)

    @pl.when(f == 0)
    def _():
        xb_ref[...] = x_ref[...].astype(BF16)
        acc_ref[...] = jnp.zeros_like(acc_ref)

    xb = xb_ref[...]
    gate = _dot(xb, wg_ref[...])
    up = _dot(xb, wu_ref[...])
    h = gate * _sigmoid(gate) * up
    acc_ref[...] += _dot(h.astype(BF16), wd_ref[...])

    @pl.when(f == pl.num_programs(1) - 1)
    def _():
        r = alpha * x_ref[...] + 0.5 * acc_ref[...]
        o_ref[...] = _layer_norm(r, g_ref[...], b_ref[...])


def _ffn_ln(x, w_up, w_down, ln_g, ln_b, layer, ln_idx, alpha, tf):
    m, d = x.shape
    fp = w_down.shape[1]
    nf = fp // tf
    tm = _pick_tile(m, (512, 256, 128, 64, 32, 16, 8))
    return pl.pallas_call(
        functools.partial(_ffn_ln_kernel, alpha=alpha),
        grid=(m // tm, nf),
        in_specs=[
            pl.BlockSpec((tm, d), lambda i, f: (i, 0)),
            pl.BlockSpec((None, d, tf), lambda i, f: (layer, 0, f)),
            pl.BlockSpec((None, d, tf), lambda i, f: (layer, 0, f + nf)),
            pl.BlockSpec((None, tf, d), lambda i, f: (layer, f, 0)),
            pl.BlockSpec((None, 1, d), lambda i, f: (ln_idx, 0, 0)),
            pl.BlockSpec((None, 1, d), lambda i, f: (ln_idx, 0, 0)),
        ],
        out_specs=pl.BlockSpec((tm, d), lambda i, f: (i, 0)),
        out_shape=jax.ShapeDtypeStruct((m, d), F32),
        scratch_shapes=[pltpu.VMEM((tm, d), BF16), pltpu.VMEM((tm, d), F32)],
        compiler_params=_cparams("parallel", "arbitrary"),
        name="ffn_ln",
    )(x, w_up, w_up, w_down, ln_g, ln_b)


def _proj_kernel(x_ref, w_ref, o_ref, xb_ref):
    @pl.when(pl.program_id(1) == 0)
    def _():
        xb_ref[...] = x_ref[...].astype(BF16)

    o_ref[...] = _dot(xb_ref[...], w_ref[...]).astype(o_ref.dtype)


def _proj(x, w, layer, col0, n, out_dtype=F32):
    m, d = x.shape
    tm = _pick_tile(m, (1024, 512, 256, 128, 64, 32, 16, 8))
    tn = _pick_tile(n, (512, 256, 128))
    assert col0 % tn == 0
    c0 = col0 // tn
    return pl.pallas_call(
        _proj_kernel,
        grid=(m // tm, n // tn),
        in_specs=[
            pl.BlockSpec((tm, d), lambda i, j: (i, 0)),
            pl.BlockSpec((None, d, tn), lambda i, j: (layer, 0, c0 + j)),
        ],
        out_specs=pl.BlockSpec((tm, tn), lambda i, j: (i, j)),
        out_shape=jax.ShapeDtypeStruct((m, n), out_dtype),
        scratch_shapes=[pltpu.VMEM((tm, d), BF16)],
        compiler_params=_cparams("parallel", "arbitrary"),
        name="proj",
    )(x, w)


def _fgate_kernel(x_ref, w_ref, b_ref, o_ref):
    z = _dot(x_ref[...].astype(BF16), w_ref[...]) + b_ref[...]
    o_ref[...] = _log_sigmoid(z)


def _fgate(x, wf, bf, layer):
    m, d = x.shape
    tm = _pick_tile(m, (1024, 512, 256, 128, 64, 32, 16, 8))
    return pl.pallas_call(
        _fgate_kernel,
        grid=(m // tm,),
        in_specs=[
            pl.BlockSpec((tm, d), lambda i: (i, 0)),
            pl.BlockSpec((None, d, LANES), lambda i: (layer, 0, 0)),
            pl.BlockSpec((None, 1, LANES), lambda i: (layer, 0, 0)),
        ],
        out_specs=pl.BlockSpec((tm, LANES), lambda i: (i, 0)),
        out_shape=jax.ShapeDtypeStruct((m, LANES), F32),
        compiler_params=_cparams("parallel"),
        name="fgate",
    )(x, wf, bf)


def _cumsum_lanes_kernel(x_ref, o_ref):
    x = x_ref[...]
    n = x.shape[-1]
    lane = lax.broadcasted_iota(jnp.int32, x.shape, 1)
    s = 1
    while s < n:
        x = x + jnp.where(lane >= s, pltpu.roll(x, s, 1), 0.0)
        s *= 2
    o_ref[...] = x


def _cumsum_lanes(x):
    r, n = x.shape
    rb = _pick_tile(r, (64, 32, 16, 8))
    return pl.pallas_call(
        _cumsum_lanes_kernel,
        grid=(r // rb,),
        in_specs=[pl.BlockSpec((rb, n), lambda i: (i, 0))],
        out_specs=pl.BlockSpec((rb, n), lambda i: (i, 0)),
        out_shape=jax.ShapeDtypeStruct((r, n), F32),
        compiler_params=_cparams("parallel"),
        name="cumsum_lanes",
    )(x)


def _head_column(c_ref, h):
    c = c_ref[...]
    lane = lax.broadcasted_iota(jnp.int32, c.shape, 1)
    return jnp.sum(jnp.where(lane == h, c, 0.0), axis=1, keepdims=True)


def _fox_attn_kernel(q_ref, k_ref, v_ref, crow_ref, ccol_ref, o_ref, kb_ref, vb_ref, *, tq, tk, scale):
    h = pl.program_id(1)
    qi = pl.program_id(2)

    @pl.when(qi == 0)
    def _():
        kb_ref[...] = k_ref[...].astype(BF16)
        vb_ref[...] = v_ref[...].astype(BF16)

    q = q_ref[...].astype(BF16)
    ci = _head_column(ccol_ref, h)
    row = qi * tq + lax.broadcasted_iota(jnp.int32, (tq, tk), 0)
    col = lax.broadcasted_iota(jnp.int32, (tq, tk), 1)

    def body(j, carry):
        m, l, acc = carry
        start = pl.multiple_of(j * tk, tk)
        kj = kb_ref[pl.ds(start, tk), :]
        vj = vb_ref[pl.ds(start, tk), :]
        cj = crow_ref[j]
        s = _dot_nt(q, kj) * scale + (ci - cj)
        s = jnp.where(col + start <= row, s, NEG_INF)
        m_new = jnp.maximum(m, jnp.max(s, axis=1, keepdims=True))
        a = jnp.exp(m - m_new)
        p = jnp.exp(s - m_new)
        l = a * l + jnp.sum(p, axis=1, keepdims=True)
        acc = a * acc + _dot(p.astype(BF16), vj)
        return m_new, l, acc

    n_kv = (qi * tq + tq + tk - 1) // tk
    m0 = jnp.full((tq, 1), NEG_INF, F32)
    l0 = jnp.zeros((tq, 1), F32)
    acc0 = jnp.zeros((tq, LANES), F32)
    _, l, acc = lax.fori_loop(0, n_kv, body, (m0, l0, acc0))
    o_ref[...] = (acc / l).astype(o_ref.dtype)


def _fox_attn(q, k, v, c_row, c_col, batch, seq, heads):
    m, d = q.shape
    tq = _pick_tile(seq, (256, 128, 64, 32, 16, 8))
    tk = _pick_tile(seq, (512, 256, 128))
    nq = seq // tq
    nk = seq // tk
    c_row = c_row.reshape(batch * heads, nk, 1, tk)
    return pl.pallas_call(
        functools.partial(_fox_attn_kernel, tq=tq, tk=tk, scale=LANES ** -0.5),
        grid=(batch, heads, nq),
        in_specs=[
            pl.BlockSpec((tq, LANES), lambda b, h, i: (b * nq + i, h)),
            pl.BlockSpec((seq, LANES), lambda b, h, i: (b, h)),
            pl.BlockSpec((seq, LANES), lambda b, h, i: (b, h)),
            pl.BlockSpec((None, nk, 1, tk), lambda b, h, i: (b * heads + h, 0, 0, 0)),
            pl.BlockSpec((tq, heads), lambda b, h, i: (b * nq + i, 0)),
        ],
        out_specs=pl.BlockSpec((tq, LANES), lambda b, h, i: (b * nq + i, h)),
        out_shape=jax.ShapeDtypeStruct((m, d), BF16),
        scratch_shapes=[pltpu.VMEM((seq, LANES), BF16), pltpu.VMEM((seq, LANES), BF16)],
        compiler_params=_cparams("parallel", "parallel", "arbitrary"),
        name="fox_attn",
    )(q, k, v, c_row, c_col)


def _fox_attn_sample_kernel(q_ref, kn_ref, vn_ref, kp_ref, vp_ref, cpast_ref, cnew_ref, ccol_ref,
                            o_ref, *, scale):
    h = pl.program_id(1)
    q = q_ref[...].astype(BF16)
    ts = q.shape[0]
    ci = _head_column(ccol_ref, h)
    sp = _dot_nt(q, kp_ref[...].astype(BF16)) * scale + (ci - cpast_ref[...])
    sn = _dot_nt(q, kn_ref[...].astype(BF16)) * scale + (ci - cnew_ref[...])
    row = lax.broadcasted_iota(jnp.int32, (ts, ts), 0)
    col = lax.broadcasted_iota(jnp.int32, (ts, ts), 1)
    sn = jnp.where(col <= row, sn, NEG_INF)
    m = jnp.maximum(jnp.max(sp, axis=1, keepdims=True), jnp.max(sn, axis=1, keepdims=True))
    pp = jnp.exp(sp - m)
    pn = jnp.exp(sn - m)
    l = jnp.sum(pp, axis=1, keepdims=True) + jnp.sum(pn, axis=1, keepdims=True)
    acc = _dot(pp.astype(BF16), vp_ref[...].astype(BF16)) + _dot(pn.astype(BF16), vn_ref[...].astype(BF16))
    o_ref[...] = (acc / l).astype(o_ref.dtype)


def _fox_attn_sample(q, k, v, k_past, v_past, c_past, c_new, c_col, batch, ts, past, heads):
    m, d = q.shape
    c_past = c_past.reshape(batch * heads, 1, past)
    c_new = c_new.reshape(batch * heads, 1, ts)
    return pl.pallas_call(
        functools.partial(_fox_attn_sample_kernel, scale=LANES ** -0.5),
        grid=(batch, heads),
        in_specs=[
            pl.BlockSpec((ts, LANES), lambda b, h: (b, h)),
            pl.BlockSpec((ts, LANES), lambda b, h: (b, h)),
            pl.BlockSpec((ts, LANES), lambda b, h: (b, h)),
            pl.BlockSpec((past, LANES), lambda b, h: (b, h)),
            pl.BlockSpec((past, LANES), lambda b, h: (b, h)),
            pl.BlockSpec((None, 1, past), lambda b, h: (b * heads + h, 0, 0)),
            pl.BlockSpec((None, 1, ts), lambda b, h: (b * heads + h, 0, 0)),
            pl.BlockSpec((ts, heads), lambda b, h: (b, 0)),
        ],
        out_specs=pl.BlockSpec((ts, LANES), lambda b, h: (b, h)),
        out_shape=jax.ShapeDtypeStruct((m, d), BF16),
        compiler_params=_cparams("parallel", "parallel"),
        name="fox_attn_sample",
    )(q, k, v, k_past, v_past, c_past, c_new, c_col)


def _cumsum_rows(x):
    n = x.shape[0]
    row = lax.broadcasted_iota(jnp.int32, x.shape, 0)
    s = 1
    while s < n:
        x = x + jnp.where(row >= s, pltpu.roll(x, s, 0), 0.0)
        s *= 2
    return x


def _group_row(b, group, offset):
    n = b.shape[0]
    parts = [jnp.broadcast_to(b[g0 + offset:g0 + offset + 1, :], (group, b.shape[1]))
             for g0 in range(0, n, group)]
    return parts[0] if len(parts) == 1 else jnp.concatenate(parts, axis=0)


def _hgrn_intra(qf, kk, b):
    n = qf.shape[0]
    row = lax.broadcasted_iota(jnp.int32, (n, n), 0)
    col = lax.broadcasted_iota(jnp.int32, (n, n), 1)
    sub = min(HGRN_SUB, n)
    shift = sub.bit_length() - 1
    bmid = _group_row(b, sub, sub // 2)
    qd = qf * jnp.exp(b - bmid)
    kd = kk * jnp.exp(bmid - b)
    a = jnp.where(((row >> shift) == (col >> shift)) & (col <= row),
                  _dot_nt(qd.astype(BF16), kd.astype(BF16)), 0.0)
    rowk = lax.broadcasted_iota(jnp.int32, qf.shape, 0)
    s = sub
    while s < n:
        shift += 1
        ref = _group_row(b, 2 * s, s - 1)
        upper = ((rowk >> (shift - 1)) & 1) == 1
        ql = jnp.where(upper, qf * jnp.exp(jnp.minimum(b - ref, 0.0)), 0.0)
        kl = jnp.where(upper, 0.0, kk * jnp.exp(jnp.minimum(ref - b, 0.0)))
        a = a + jnp.where((row >> shift) == (col >> shift),
                          _dot_nt(ql.astype(BF16), kl.astype(BF16)), 0.0)
        s *= 2
    return a


def _hgrn_kernel(*refs, chunk, n_chunks, has_s0, layer, kscale):
    if has_s0:
        q_ref, z_ref, i_ref, g_ref, lbp_ref, ng_ref, s0_ref, o_ref, s_ref = refs
    else:
        q_ref, z_ref, i_ref, g_ref, lbp_ref, ng_ref, o_ref, s_ref = refs
        s0_ref = None

    lbp = lbp_ref[...]
    e = jnp.exp(lbp - jnp.max(lbp, axis=0, keepdims=True))
    soft = e / jnp.sum(e, axis=0, keepdims=True)
    r = lax.broadcasted_iota(jnp.int32, lbp.shape, 0)
    lb = jnp.sum(jnp.where((r >= 1) & (r <= layer), soft, 0.0), axis=0, keepdims=True)
    log_lb = jnp.log(lb)
    log_1mlb = jnp.log1p(-lb)
    ng = ng_ref[...]

    def step(c, st):
        start = pl.multiple_of(c * chunk, chunk)
        rows = pl.ds(start, chunk)
        q = q_ref[rows, :]
        z = z_ref[rows, :]
        v = i_ref[rows, :]
        go = g_ref[rows, :]
        t2 = log_1mlb + _log_sigmoid(z)
        g = jnp.maximum(log_lb, t2) + jnp.log1p(jnp.exp(-jnp.abs(log_lb - t2)))
        kk = (1.0 - lb) * _sigmoid(-z)
        qf = q * _sigmoid(q) * kscale
        b = _cumsum_rows(g)
        vb = v.astype(BF16)
        a = _hgrn_intra(qf, kk, b)
        o = _dot(a.astype(BF16), vb) + _dot_nt((qf * jnp.exp(b)).astype(BF16), st.astype(BF16))
        blast = b[chunk - 1:chunk, :]
        kdec = kk * jnp.exp(blast - b)
        st_new = st * jnp.exp(blast) + _dot_tn(vb, kdec.astype(BF16))
        o = o * lax.rsqrt(jnp.mean(o * o, axis=-1, keepdims=True) + RMS_EPS) * ng
        o = o * (go * _sigmoid(go))
        o_ref[rows, :] = o.astype(o_ref.dtype)
        return st_new

    if has_s0:
        st0 = s0_ref[...].T
    else:
        st0 = jnp.zeros((LANES, LANES), F32)
    st = lax.fori_loop(0, n_chunks, step, st0)
    s_ref[...] = st.T


def _hgrn(proj, lb_param, norm_g, s0, batch, seq, heads, layer, mixer_idx):
    m = proj.shape[0]
    d = heads * LANES
    chunk = _pick_tile(seq, (HGRN_CHUNK, 64, 32, 16))
    assert chunk % HGRN_SUB == 0 and seq % chunk == 0
    depth = lb_param.shape[0]
    has_s0 = s0 is not None
    in_specs = [
        pl.BlockSpec((seq, LANES), lambda b, h: (b, h)),
        pl.BlockSpec((seq, LANES), lambda b, h: (b, heads + h)),
        pl.BlockSpec((seq, LANES), lambda b, h: (b, 2 * heads + h)),
        pl.BlockSpec((seq, LANES), lambda b, h: (b, 3 * heads + h)),
        pl.BlockSpec((depth, LANES), lambda b, h: (0, h)),
        pl.BlockSpec((None, 1, LANES), lambda b, h: (mixer_idx, 0, 0)),
    ]
    args = [proj, proj, proj, proj, lb_param, norm_g]
    if has_s0:
        in_specs.append(pl.BlockSpec((None, None, LANES, LANES), lambda b, h: (b, h, 0, 0)))
        args.append(s0)
    return pl.pallas_call(
        functools.partial(_hgrn_kernel, chunk=chunk, n_chunks=seq // chunk, has_s0=has_s0,
                          layer=layer, kscale=LANES ** -0.5),
        grid=(batch, heads),
        in_specs=in_specs,
        out_specs=[
            pl.BlockSpec((seq, LANES), lambda b, h: (b, h)),
            pl.BlockSpec((None, None, LANES, LANES), lambda b, h: (b, h, 0, 0)),
        ],
        out_shape=[
            jax.ShapeDtypeStruct((m, d), BF16),
            jax.ShapeDtypeStruct((batch, heads, LANES, LANES), F32),
        ],
        compiler_params=_cparams("parallel", "parallel"),
        name="hgrn",
    )(*args)


def _outproj_ln_kernel(o_ref, w_ref, x_ref, g_ref, b_ref, y_ref, *, alpha):
    r = alpha * x_ref[...] + _dot(o_ref[...], w_ref[...])
    y_ref[...] = _layer_norm(r, g_ref[...], b_ref[...])


def _outproj_ln(o, w, x, ln_g, ln_b, layer, ln_idx, alpha):
    m, d = x.shape
    tm = _pick_tile(m, (512, 256, 128, 64, 32, 16))
    return pl.pallas_call(
        functools.partial(_outproj_ln_kernel, alpha=alpha),
        grid=(m // tm,),
        in_specs=[
            pl.BlockSpec((tm, d), lambda i: (i, 0)),
            pl.BlockSpec((None, d, d), lambda i: (layer, 0, 0)),
            pl.BlockSpec((tm, d), lambda i: (i, 0)),
            pl.BlockSpec((None, 1, d), lambda i: (ln_idx, 0, 0)),
            pl.BlockSpec((None, 1, d), lambda i: (ln_idx, 0, 0)),
        ],
        out_specs=pl.BlockSpec((tm, d), lambda i: (i, 0)),
        out_shape=jax.ShapeDtypeStruct((m, d), F32),
        compiler_params=_cparams("parallel"),
        name="outproj_ln",
    )(o, w, x, ln_g, ln_b)


def _prep_ffn(w_up, w_down, tf):
    f = w_down.shape[1]
    fp = _round_up(f, tf)
    pad = ((0, 0), (0, 0), (0, fp - f))
    up = jnp.concatenate([jnp.pad(w_up[..., :f], pad), jnp.pad(w_up[..., f:], pad)], axis=-1)
    down = jnp.pad(w_down, ((0, 0), (0, fp - f), (0, 0)))
    return up.astype(BF16), down.astype(BF16)


def _cumsum_heads(logf_bsh):
    b, s, h = logf_bsh.shape
    sp = _round_up(s, LANES)
    x = jnp.transpose(logf_bsh, (0, 2, 1)).reshape(b * h, s)
    x = jnp.pad(x, ((0, 0), (0, sp - s)))
    return _cumsum_lanes(x)[:, :s]


def kernel(x_prompt, x_sample, cache_fox_k, cache_fox_v, cache_fox_logf, state_hgrn, ln_g, ln_b, ffn1_up, ffn1_down, ffn2_up, ffn2_down, fox_w_in, fox_b_f, fox_w_out, hgrn_w_in, hgrn_lb, hgrn_norm_g, hgrn_w_out):
    bp, tp, d = x_prompt.shape
    bs, ts, _ = x_sample.shape
    depth = ln_g.shape[0]
    heads = fox_b_f.shape[1]
    past = cache_fox_k.shape[2]
    assert d == heads * LANES and hgrn_norm_g.shape[1] == LANES
    alpha = (2 * depth) ** 0.25

    tf = 512 if ffn1_down.shape[1] >= 512 else LANES
    up1, down1 = _prep_ffn(ffn1_up, ffn1_down, tf)
    up2, down2 = _prep_ffn(ffn2_up, ffn2_down, tf)
    fox_qkv = fox_w_in[..., :3 * d].astype(BF16)
    fox_wf = jnp.pad(fox_w_in[..., 3 * d:], ((0, 0), (0, 0), (0, LANES - heads))).astype(BF16)
    fox_bf = jnp.pad(fox_b_f, ((0, 0), (0, LANES - heads)))[:, None, :]
    fox_out = fox_w_out.astype(BF16)
    hgrn_in = hgrn_w_in.astype(BF16)
    hgrn_out = hgrn_w_out.astype(BF16)
    hgrn_ng = hgrn_norm_g[:, None, :]
    g3 = ln_g.reshape(depth * 3, 1, d)
    b3 = ln_b.reshape(depth * 3, 1, d)

    xp = x_prompt.reshape(bp * tp, d)
    xs = x_sample.reshape(bs * ts, d)
    fox_p, fox_s, hg_p, hg_s = [], [], [], []

    for i in range(depth):
        j = i // 2
        xp = _ffn_ln(xp, up1, down1, g3, b3, i, 3 * i, alpha, tf)
        xs = _ffn_ln(xs, up1, down1, g3, b3, i, 3 * i, alpha, tf)
        if i % 2 == 0:
            q = _proj(xp, fox_qkv, j, 0, d)
            k = _proj(xp, fox_qkv, j, d, d)
            v = _proj(xp, fox_qkv, j, 2 * d, d)
            logf = _fgate(xp, fox_wf, fox_bf, j)[:, :heads]
            c_row = _cumsum_heads(logf.reshape(bp, tp, heads))
            c_col = jnp.transpose(c_row.reshape(bp, heads, tp), (0, 2, 1)).reshape(bp * tp, heads)
            op = _fox_attn(q, k, v, c_row, c_col, bp, tp, heads)
            fox_p.append((k.reshape(bp, tp, heads, LANES), v.reshape(bp, tp, heads, LANES),
                          logf.reshape(bp, tp, heads)))
            q = _proj(xs, fox_qkv, j, 0, d)
            k = _proj(xs, fox_qkv, j, d, d)
            v = _proj(xs, fox_qkv, j, 2 * d, d)
            logf = _fgate(xs, fox_wf, fox_bf, j)[:, :heads]
            logf_all = jnp.concatenate([cache_fox_logf[j].astype(F32), logf.reshape(bs, ts, heads)], axis=1)
            c_all = _cumsum_heads(logf_all)
            c_new = c_all[:, past:]
            c_col = jnp.transpose(c_new.reshape(bs, heads, ts), (0, 2, 1)).reshape(bs * ts, heads)
            os_ = _fox_attn_sample(q, k, v, cache_fox_k[j].reshape(bs * past, d),
                                   cache_fox_v[j].reshape(bs * past, d),
                                   c_all[:, :past], c_new, c_col, bs, ts, past, heads)
            fox_s.append((k.reshape(bs, ts, heads, LANES), v.reshape(bs, ts, heads, LANES),
                          logf.reshape(bs, ts, heads)))
            w_out = fox_out
        else:
            pp = _proj(xp, hgrn_in, j, 0, 4 * d)
            op, sp = _hgrn(pp, hgrn_lb, hgrn_ng, None, bp, tp, heads, i, j)
            ps = _proj(xs, hgrn_in, j, 0, 4 * d)
            os_, ss = _hgrn(ps, hgrn_lb, hgrn_ng, state_hgrn[j], bs, ts, heads, i, j)
            hg_p.append(sp)
            hg_s.append(ss)
            w_out = hgrn_out
        xp = _outproj_ln(op, w_out, xp, g3, b3, j, 3 * i + 1, alpha)
        xs = _outproj_ln(os_, w_out, xs, g3, b3, j, 3 * i + 1, alpha)
        xp = _ffn_ln(xp, up2, down2, g3, b3, i, 3 * i + 2, alpha, tf)
        xs = _ffn_ln(xs, up2, down2, g3, b3, i, 3 * i + 2, alpha, tf)

    return (xp.reshape(bp, tp, d), xs.reshape(bs, ts, d),
            jnp.stack([s[0] for s in fox_p]), jnp.stack([s[1] for s in fox_p]),
            jnp.stack([s[2] for s in fox_p]), jnp.stack(hg_p),
            jnp.stack([s[0] for s in fox_s]), jnp.stack([s[1] for s in fox_s]),
            jnp.stack([s[2] for s in fox_s]), jnp.stack(hg_s))
```

```python
import functools
import math

import jax
import jax.numpy as jnp
from jax import lax
from jax.experimental import pallas as pl
from jax.experimental.pallas import tpu as pltpu

F32 = jnp.float32
BF16 = jnp.bfloat16

LN_EPS = 1e-5
RMS_EPS = 1e-6
NEG_INF = -1e30

LANES = 128
HGRN_SUB = 16
HGRN_CHUNK = 128
VMEM_LIMIT_BYTES = 48 * 1024 * 1024


def _round_up(n, m):
    return (n + m - 1) // m * m


def _pick_tile(n, candidates):
    for c in candidates:
        if n % c == 0:
            return c
    return n


def _cparams(*semantics):
    return pltpu.CompilerParams(dimension_semantics=semantics,
                                vmem_limit_bytes=VMEM_LIMIT_BYTES)


def _dot(a, b):
    return jnp.dot(a, b, preferred_element_type=F32)


def _dot_nt(a, b):
    return lax.dot_general(a, b, (((1,), (1,)), ((), ())), preferred_element_type=F32)


def _dot_tn(a, b):
    return lax.dot_general(a, b, (((0,), (0,)), ((), ())), preferred_element_type=F32)


def _sigmoid(x):
    return 1.0 / (1.0 + jnp.exp(-x))


def _log_sigmoid(x):
    return jnp.minimum(x, 0.0) - jnp.log(1.0 + jnp.exp(-jnp.abs(x)))


def _layer_norm(r, g, b):
    mu = jnp.mean(r, axis=-1, keepdims=True)
    c = r - mu
    var = jnp.mean(c * c, axis=-1, keepdims=True)
    return c * lax.rsqrt(var + LN_EPS) * g + b


def _ffn_ln_kernel(x_ref, wg_ref, wu_ref, wd_ref, g_ref, b_ref, *rest, alpha):
    out_refs, (xb_ref, acc_ref) = rest[:-2], rest[-2:]
    f = pl.program_id(1)

    @pl.when(f == 0)
    def _():
        xb_ref[...] = x_ref[...].astype(BF16)
        acc_ref[...] = jnp.zeros_like(acc_ref)

    xb = xb_ref[...]
    gate = _dot(xb, wg_ref[...])
    up = _dot(xb, wu_ref[...])
    h = gate * _sigmoid(gate) * up
    acc_ref[...] += _dot(h.astype(BF16), wd_ref[...])

    @pl.when(f == pl.num_programs(1) - 1)
    def _():
        r = alpha * x_ref[...] + 0.5 * acc_ref[...]
        y = _layer_norm(r, g_ref[...], b_ref[...])
        for o_ref in out_refs:
            o_ref[...] = y.astype(o_ref.dtype)


def _ffn_ln(x, w_up, w_down, ln_g, ln_b, layer, ln_idx, alpha, tf, with_bf16=False):
    m, d = x.shape
    fp = w_down.shape[1]
    nf = fp // tf
    tm = _pick_tile(m, (512, 256, 128, 64, 32, 16, 8))
    out_dtypes = (F32, BF16) if with_bf16 else (F32,)
    outs = pl.pallas_call(
        functools.partial(_ffn_ln_kernel, alpha=alpha),
        grid=(m // tm, nf),
        in_specs=[
            pl.BlockSpec((tm, d), lambda i, f: (i, 0)),
            pl.BlockSpec((None, d, tf), lambda i, f: (layer, 0, f)),
            pl.BlockSpec((None, d, tf), lambda i, f: (layer, 0, f + nf)),
            pl.BlockSpec((None, tf, d), lambda i, f: (layer, f, 0)),
            pl.BlockSpec((None, 1, d), lambda i, f: (ln_idx, 0, 0)),
            pl.BlockSpec((None, 1, d), lambda i, f: (ln_idx, 0, 0)),
        ],
        out_specs=[pl.BlockSpec((tm, d), lambda i, f: (i, 0)) for _ in out_dtypes],
        out_shape=[jax.ShapeDtypeStruct((m, d), dt) for dt in out_dtypes],
        scratch_shapes=[pltpu.VMEM((tm, d), BF16), pltpu.VMEM((tm, d), F32)],
        compiler_params=_cparams("parallel", "arbitrary"),
        name="ffn_ln",
    )(x, w_up, w_up, w_down, ln_g, ln_b)
    return outs if with_bf16 else outs[0]


def _proj_kernel(x_ref, w_ref, o_ref):
    o_ref[...] = _dot(x_ref[...], w_ref[...]).astype(o_ref.dtype)


def _proj(xb, w, layer, n, out_dtype=F32):
    m, d = xb.shape
    tm = _pick_tile(m, (2048, 1024, 512, 256, 128, 64, 32, 16))
    tn = _pick_tile(n, (512, 256, 128))
    return pl.pallas_call(
        _proj_kernel,
        grid=(m // tm, n // tn),
        in_specs=[
            pl.BlockSpec((tm, d), lambda i, j: (i, 0)),
            pl.BlockSpec((None, d, tn), lambda i, j: (layer, 0, j)),
        ],
        out_specs=pl.BlockSpec((tm, tn), lambda i, j: (i, j)),
        out_shape=jax.ShapeDtypeStruct((m, n), out_dtype),
        compiler_params=_cparams("parallel", "arbitrary"),
        name="proj",
    )(xb, w)


def _fox_proj_kernel(x_ref, w_ref, wf_ref, bf_ref, *rest, nq):
    q_ref, k_ref, v_ref, lf_ref = rest[-4:]
    j = pl.program_id(1)

    @pl.when(j < nq)
    def _():
        q_ref[...] = _dot(x_ref[...], w_ref[...]).astype(q_ref.dtype)

    @pl.when((j >= nq) & (j < 2 * nq))
    def _():
        k_ref[...] = _dot(x_ref[...], w_ref[...])

    @pl.when((j >= 2 * nq) & (j < 3 * nq))
    def _():
        v_ref[...] = _dot(x_ref[...], w_ref[...])

    @pl.when(j == 3 * nq)
    def _():
        lf_ref[...] = _log_sigmoid(_dot(x_ref[...], wf_ref[...]) + bf_ref[...])


def _fox_proj(xb, w_qkv, wf, bf, layer, n_layers, kv_stacks):
    m, d = xb.shape
    tm = _pick_tile(m, (1024, 512, 256, 128, 64, 32, 16))
    tn = _pick_tile(d, (512, 256, 128))
    nq = d // tn
    in_specs = [
        pl.BlockSpec((tm, d), lambda i, j: (i, 0)),
        pl.BlockSpec((None, d, tn), lambda i, j: (layer, 0, jnp.minimum(j, 3 * nq - 1))),
        pl.BlockSpec((None, d, LANES), lambda i, j: (layer, 0, 0)),
        pl.BlockSpec((None, 1, LANES), lambda i, j: (layer, 0, 0)),
    ]
    args = [xb, w_qkv, wf, bf]
    aliases = {}
    if kv_stacks is not None:
        in_specs += [pl.BlockSpec(memory_space=pl.ANY), pl.BlockSpec(memory_space=pl.ANY)]
        args += list(kv_stacks)
        aliases = {4: 1, 5: 2}
    return pl.pallas_call(
        functools.partial(_fox_proj_kernel, nq=nq),
        grid=(m // tm, 3 * nq + 1),
        in_specs=in_specs,
        out_specs=[
            pl.BlockSpec((tm, tn), lambda i, j: (i, jnp.minimum(j, nq - 1))),
            pl.BlockSpec((None, tm, tn), lambda i, j: (layer, i, jnp.clip(j - nq, 0, nq - 1))),
            pl.BlockSpec((None, tm, tn), lambda i, j: (layer, i, jnp.clip(j - 2 * nq, 0, nq - 1))),
            pl.BlockSpec((tm, LANES), lambda i, j: (i, 0)),
        ],
        out_shape=[
            jax.ShapeDtypeStruct((m, d), BF16),
            jax.ShapeDtypeStruct((n_layers, m, d), F32),
            jax.ShapeDtypeStruct((n_layers, m, d), F32),
            jax.ShapeDtypeStruct((m, LANES), F32),
        ],
        input_output_aliases=aliases,
        compiler_params=_cparams("parallel", "arbitrary"),
        name="fox_proj",
    )(*args)


def _cumsum_lanes_kernel(x_ref, o_ref):
    x = x_ref[...]
    n = x.shape[-1]
    lane = lax.broadcasted_iota(jnp.int32, x.shape, 1)
    s = 1
    while s < n:
        x = x + jnp.where(lane >= s, pltpu.roll(x, s, 1), 0.0)
        s *= 2
    o_ref[...] = x


def _cumsum_lanes(x):
    r, n = x.shape
    rb = _pick_tile(r, (64, 32, 16, 8))
    return pl.pallas_call(
        _cumsum_lanes_kernel,
        grid=(r // rb,),
        in_specs=[pl.BlockSpec((rb, n), lambda i: (i, 0))],
        out_specs=pl.BlockSpec((rb, n), lambda i: (i, 0)),
        out_shape=jax.ShapeDtypeStruct((r, n), F32),
        compiler_params=_cparams("parallel"),
        name="cumsum_lanes",
    )(x)


def _head_column(c, h):
    lane = lax.broadcasted_iota(jnp.int32, c.shape, 1)
    return jnp.sum(jnp.where(lane == h, c, 0.0), axis=1, keepdims=True)


_LOG2E = math.log2(math.e)


def _softmax_tile(u, ci, m, l, acc, v, a_coef):
    m_new = jnp.maximum(m, jnp.max(u, axis=1, keepdims=True) + ci)
    alpha = jnp.exp2(a_coef * (m - m_new))
    p = jnp.exp2(a_coef * (u - (m_new - ci)))
    l = alpha * l + jnp.sum(p, axis=1, keepdims=True)
    acc = alpha * acc + _dot(p.astype(BF16), v)
    return m_new, l, acc


def _fox_attn_kernel(q_ref, k_ref, v_ref, crow_ref, ccol_ref, o_ref, kb_ref, vb_ref, *,
                     tq, scale, hp):
    g = pl.program_id(1)
    a_coef = scale * _LOG2E
    inv_scale = 1.0 / scale
    seq = q_ref.shape[0]
    kb_ref[...] = k_ref[...].astype(BF16)
    vb_ref[...] = v_ref[...].astype(BF16)
    cc = ccol_ref[...]
    row = lax.broadcasted_iota(jnp.int32, (tq, tq), 0)
    col = lax.broadcasted_iota(jnp.int32, (tq, tq), 1)
    for hh in range(hp):
        sl = slice(hh * LANES, (hh + 1) * LANES)
        ci_all = _head_column(cc, g * hp + hh) * inv_scale
        cj_all = crow_ref[hh] * inv_scale
        for r0 in range(0, seq, tq):
            w = r0 + tq
            u = _dot_nt(q_ref[r0:w, sl], kb_ref[0:w, sl]) - cj_all[:, 0:w]
            diag = jnp.where(col <= row, u[:, r0:w], NEG_INF)
            u = diag if r0 == 0 else jnp.concatenate([u[:, 0:r0], diag], axis=1)
            ci = ci_all[r0:w, :]
            m = jnp.max(u, axis=1, keepdims=True) + ci
            p = jnp.exp2(a_coef * (u - (m - ci)))
            l = jnp.sum(p, axis=1, keepdims=True)
            o = _dot(p.astype(BF16), vb_ref[0:w, sl]) / l
            o_ref[r0:w, sl] = o.astype(o_ref.dtype)


def _fox_attn(q, k_stack, v_stack, layer, c_row, c_col, batch, seq, heads):
    m, d = q.shape
    hp = 2 if heads % 2 == 0 else 1
    tq = _pick_tile(seq, (256, 128))
    ng = heads // hp
    w = hp * LANES
    c_row = c_row.reshape(batch * ng, hp, 1, seq)
    return pl.pallas_call(
        functools.partial(_fox_attn_kernel, tq=tq, scale=LANES ** -0.5, hp=hp),
        grid=(batch, ng),
        in_specs=[
            pl.BlockSpec((seq, w), lambda b, g: (b, g)),
            pl.BlockSpec((None, seq, w), lambda b, g: (layer, b, g)),
            pl.BlockSpec((None, seq, w), lambda b, g: (layer, b, g)),
            pl.BlockSpec((None, hp, 1, seq), lambda b, g: (b * ng + g, 0, 0, 0)),
            pl.BlockSpec((seq, heads), lambda b, g: (b, 0)),
        ],
        out_specs=pl.BlockSpec((seq, w), lambda b, g: (b, g)),
        out_shape=jax.ShapeDtypeStruct((m, d), BF16),
        scratch_shapes=[pltpu.VMEM((seq, w), BF16), pltpu.VMEM((seq, w), BF16)],
        compiler_params=_cparams("parallel", "parallel"),
        name="fox_attn",
    )(q, k_stack, v_stack, c_row, c_col)


def _fox_attn_sample_kernel(q_ref, kn_ref, vn_ref, kp_ref, vp_ref, cpast_ref, cnew_ref, ccol_ref,
                            o_ref, *, scale, hp):
    g = pl.program_id(1)
    a_coef = scale * _LOG2E
    inv_scale = 1.0 / scale
    ts = q_ref.shape[0]
    cc = ccol_ref[...]
    row = lax.broadcasted_iota(jnp.int32, (ts, ts), 0)
    col = lax.broadcasted_iota(jnp.int32, (ts, ts), 1)
    for hh in range(hp):
        sl = slice(hh * LANES, (hh + 1) * LANES)
        q = q_ref[:, sl]
        ci = _head_column(cc, g * hp + hh) * inv_scale
        carry = (jnp.full((ts, 1), NEG_INF, F32), jnp.zeros((ts, 1), F32), jnp.zeros((ts, LANES), F32))
        up = _dot_nt(q, kp_ref[:, sl].astype(BF16)) - cpast_ref[hh] * inv_scale
        carry = _softmax_tile(up, ci, *carry, vp_ref[:, sl].astype(BF16), a_coef)
        un = _dot_nt(q, kn_ref[:, sl].astype(BF16)) - cnew_ref[hh] * inv_scale
        un = jnp.where(col <= row, un, NEG_INF)
        _, l, acc = _softmax_tile(un, ci, *carry, vn_ref[:, sl].astype(BF16), a_coef)
        o_ref[:, sl] = (acc / l).astype(o_ref.dtype)


def _fox_attn_sample(q, k_stack, v_stack, layer, k_past, v_past, c_past, c_new, c_col,
                     batch, ts, past, heads):
    m, d = q.shape
    hp = 4 if heads % 4 == 0 else 1
    ng = heads // hp
    w = hp * LANES
    c_past = c_past.reshape(batch * ng, hp, 1, past)
    c_new = c_new.reshape(batch * ng, hp, 1, ts)
    return pl.pallas_call(
        functools.partial(_fox_attn_sample_kernel, scale=LANES ** -0.5, hp=hp),
        grid=(batch, ng),
        in_specs=[
            pl.BlockSpec((ts, w), lambda b, g: (b, g)),
            pl.BlockSpec((None, ts, w), lambda b, g: (layer, b, g)),
            pl.BlockSpec((None, ts, w), lambda b, g: (layer, b, g)),
            pl.BlockSpec((past, w), lambda b, g: (b, g)),
            pl.BlockSpec((past, w), lambda b, g: (b, g)),
            pl.BlockSpec((None, hp, 1, past), lambda b, g: (b * ng + g, 0, 0, 0)),
            pl.BlockSpec((None, hp, 1, ts), lambda b, g: (b * ng + g, 0, 0, 0)),
            pl.BlockSpec((ts, heads), lambda b, g: (b, 0)),
        ],
        out_specs=pl.BlockSpec((ts, w), lambda b, g: (b, g)),
        out_shape=jax.ShapeDtypeStruct((m, d), BF16),
        compiler_params=_cparams("parallel", "parallel"),
        name="fox_attn_sample",
    )(q, k_stack, v_stack, k_past, v_past, c_past, c_new, c_col)


def _expand_rows(f, reps):
    parts = [jnp.broadcast_to(f[r:r + 1, :], (reps, f.shape[1])) for r in range(f.shape[0])]
    return parts[0] if len(parts) == 1 else jnp.concatenate(parts, axis=0)


def _take_rows(f, idx):
    parts = [f[r:r + 1, :] for r in idx]
    return parts[0] if len(parts) == 1 else jnp.concatenate(parts, axis=0)


def _hgrn_chunk(q, z, v, go, st, bscr_ref, lb, log_lb, log_1mlb, ng, kscale):
    n = q.shape[0]
    nblk = n // HGRN_SUB
    e = jnp.exp(-jnp.abs(z))
    d1 = 1.0 + e
    r1 = 1.0 / d1
    ls = jnp.minimum(z, 0.0) - jnp.log(d1)
    kk = (1.0 - lb) * jnp.where(z > 0.0, e * r1, r1)
    t2 = log_1mlb + ls
    g = jnp.maximum(log_lb, t2) + jnp.log(1.0 + jnp.exp(-jnp.abs(log_lb - t2)))
    qf = q * _sigmoid(q) * kscale

    row = lax.broadcasted_iota(jnp.int32, (n, n), 0)
    col = lax.broadcasted_iota(jnp.int32, (n, n), 1)
    tri = jnp.where(col <= row, 1.0, 0.0).astype(BF16)
    g_hi = g.astype(BF16)
    g_r1 = g - g_hi.astype(F32)
    g_mid = g_r1.astype(BF16)
    g_lo = (g_r1 - g_mid.astype(F32)).astype(BF16)
    b = _dot(tri, g_hi) + _dot(tri, g_mid) + _dot(tri, g_lo)

    bscr_ref[...] = b
    bmid = bscr_ref[pl.ds(HGRN_SUB // 2, nblk, stride=HGRN_SUB), :]
    bend = bscr_ref[pl.ds(HGRN_SUB - 1, nblk, stride=HGRN_SUB), :]
    blast = bend[nblk - 1:nblk, :]
    bmid_rows = _expand_rows(bmid, HGRN_SUB)
    qd = qf * jnp.exp(b - bmid_rows)
    kd = kk * jnp.exp(bmid_rows - b)

    blk = lax.broadcasted_iota(jnp.int32, (nblk, LANES), 0)
    a = None
    s = nblk // 2
    while s >= 1:
        half_shift = s.bit_length() - 1
        ref = _take_rows(bend, [(i // (2 * s)) * 2 * s + s - 1 for i in range(nblk)])
        upper = ((blk >> half_shift) & 1) == 1
        fq = jnp.where(upper, jnp.exp(jnp.minimum(bmid - ref, 0.0)), 0.0)
        fk = jnp.where(upper, 0.0, jnp.exp(jnp.minimum(ref - bmid, 0.0)))
        ql = qd * _expand_rows(fq, HGRN_SUB)
        kl = kd * _expand_rows(fk, HGRN_SUB)
        prod = _dot_nt(ql.astype(BF16), kl.astype(BF16))
        group_shift = (2 * s * HGRN_SUB).bit_length() - 1
        a = prod if a is None else jnp.where((row >> group_shift) == (col >> group_shift), prod, a)
        s //= 2
    sub_shift = HGRN_SUB.bit_length() - 1
    pd = _dot_nt(qd.astype(BF16), kd.astype(BF16))
    on_diag = ((row >> sub_shift) == (col >> sub_shift)) & (col <= row)
    a = jnp.where(on_diag, pd, 0.0 if a is None else a)

    vb = v.astype(BF16)
    q_in = qd * _expand_rows(jnp.exp(bmid), HGRN_SUB)
    k_out = kd * _expand_rows(jnp.exp(blast - bmid), HGRN_SUB)
    o = _dot(a.astype(BF16), vb) + _dot_nt(q_in.astype(BF16), st.astype(BF16))
    st_new = st * jnp.exp(blast) + _dot_tn(vb, k_out.astype(BF16))
    o = o * lax.rsqrt(jnp.mean(o * o, axis=-1, keepdims=True) + RMS_EPS) * ng
    o = o * (go * _sigmoid(go))
    return o, st_new


def _hgrn_kernel(*refs, chunk, n_chunks, has_s0, layer, kscale, hp):
    if has_s0:
        q_ref, z_ref, i_ref, g_ref, lbp_ref, ng_ref, s0_ref, o_ref, s_ref, bscr_ref = refs
    else:
        q_ref, z_ref, i_ref, g_ref, lbp_ref, ng_ref, o_ref, s_ref, bscr_ref = refs
        s0_ref = None

    lbp = lbp_ref[...]
    e = jnp.exp(lbp - jnp.max(lbp, axis=0, keepdims=True))
    soft = e / jnp.sum(e, axis=0, keepdims=True)
    r = lax.broadcasted_iota(jnp.int32, lbp.shape, 0)
    lb_all = jnp.sum(jnp.where((r >= 1) & (r <= layer), soft, 0.0), axis=0, keepdims=True)
    ng = ng_ref[...]

    group = bscr_ref.shape[1]

    def step(t, states):
        states = list(states)
        for u in range(group):
            rows = pl.ds(pl.multiple_of((t * group + u) * chunk, chunk), chunk)
            for hh in range(hp):
                sl = slice(hh * LANES, (hh + 1) * LANES)
                lb = lb_all[:, sl]
                o, states[hh] = _hgrn_chunk(
                    q_ref[rows, sl], z_ref[rows, sl], i_ref[rows, sl], g_ref[rows, sl], states[hh],
                    bscr_ref.at[hh, u], lb, jnp.log(lb), jnp.log1p(-lb), ng, kscale)
                o_ref[rows, sl] = o.astype(o_ref.dtype)
        return tuple(states)

    if has_s0:
        init = tuple(s0_ref[hh].T for hh in range(hp))
    else:
        init = tuple(jnp.zeros((LANES, LANES), F32) for _ in range(hp))
    if n_chunks == group:
        states = step(0, init)
    else:
        states = lax.fori_loop(0, n_chunks // group, step, init)
    for hh in range(hp):
        s_ref[hh] = states[hh].T


def _hgrn(proj, lb_param, norm_g, s0, batch, seq, heads, layer, mixer_idx):
    m = proj.shape[0]
    d = heads * LANES
    chunk = _pick_tile(seq, (HGRN_CHUNK, 64, 32, 16))
    assert chunk % HGRN_SUB == 0 and seq % chunk == 0
    depth = lb_param.shape[0]
    has_s0 = s0 is not None
    hp = _pick_tile(heads, (2, 1) if seq > chunk else (4, 2, 1))
    group = _pick_tile(seq // chunk, (4, 2, 1))
    ng = heads // hp
    w = hp * LANES
    in_specs = [
        pl.BlockSpec((seq, w), lambda b, g: (b, g)),
        pl.BlockSpec((seq, w), lambda b, g: (b, ng + g)),
        pl.BlockSpec((seq, w), lambda b, g: (b, 2 * ng + g)),
        pl.BlockSpec((seq, w), lambda b, g: (b, 3 * ng + g)),
        pl.BlockSpec((depth, w), lambda b, g: (0, g)),
        pl.BlockSpec((None, 1, LANES), lambda b, g: (mixer_idx, 0, 0)),
    ]
    args = [proj, proj, proj, proj, lb_param, norm_g]
    if has_s0:
        in_specs.append(pl.BlockSpec((None, hp, LANES, LANES), lambda b, g: (b, g, 0, 0)))
        args.append(s0)
    return pl.pallas_call(
        functools.partial(_hgrn_kernel, chunk=chunk, n_chunks=seq // chunk, has_s0=has_s0,
                          layer=layer, kscale=LANES ** -0.5, hp=hp),
        grid=(batch, ng),
        in_specs=in_specs,
        out_specs=[
            pl.BlockSpec((seq, w), lambda b, g: (b, g)),
            pl.BlockSpec((None, hp, LANES, LANES), lambda b, g: (b, g, 0, 0)),
        ],
        out_shape=[
            jax.ShapeDtypeStruct((m, d), BF16),
            jax.ShapeDtypeStruct((batch, heads, LANES, LANES), F32),
        ],
        scratch_shapes=[pltpu.VMEM((hp, group, chunk, LANES), F32)],
        compiler_params=_cparams("parallel", "parallel"),
        name="hgrn",
    )(*args)


def _outproj_ln_kernel(o_ref, w_ref, x_ref, g_ref, b_ref, y_ref, *, alpha):
    r = alpha * x_ref[...] + _dot(o_ref[...], w_ref[...])
    y_ref[...] = _layer_norm(r, g_ref[...], b_ref[...])


def _outproj_ln(o, w, x, ln_g, ln_b, layer, ln_idx, alpha):
    m, d = x.shape
    tm = _pick_tile(m, (512, 256, 128, 64, 32, 16))
    return pl.pallas_call(
        functools.partial(_outproj_ln_kernel, alpha=alpha),
        grid=(m // tm,),
        in_specs=[
            pl.BlockSpec((tm, d), lambda i: (i, 0)),
            pl.BlockSpec((None, d, d), lambda i: (layer, 0, 0)),
            pl.BlockSpec((tm, d), lambda i: (i, 0)),
            pl.BlockSpec((None, 1, d), lambda i: (ln_idx, 0, 0)),
            pl.BlockSpec((None, 1, d), lambda i: (ln_idx, 0, 0)),
        ],
        out_specs=pl.BlockSpec((tm, d), lambda i: (i, 0)),
        out_shape=jax.ShapeDtypeStruct((m, d), F32),
        compiler_params=_cparams("parallel"),
        name="outproj_ln",
    )(o, w, x, ln_g, ln_b)


def _prep_ffn(w_up, w_down, tf):
    f = w_down.shape[1]
    fp = _round_up(f, tf)
    pad = ((0, 0), (0, 0), (0, fp - f))
    up = jnp.concatenate([jnp.pad(w_up[..., :f], pad), jnp.pad(w_up[..., f:], pad)], axis=-1)
    down = jnp.pad(w_down, ((0, 0), (0, fp - f), (0, 0)))
    return up.astype(BF16), down.astype(BF16)


def _cumsum_heads(logf_bsh):
    b, s, h = logf_bsh.shape
    sp = _round_up(s, LANES)
    x = jnp.transpose(logf_bsh, (0, 2, 1)).reshape(b * h, s)
    x = jnp.pad(x, ((0, 0), (0, sp - s)))
    return _cumsum_lanes(x)[:, :s]


def kernel(x_prompt, x_sample, cache_fox_k, cache_fox_v, cache_fox_logf, state_hgrn, ln_g, ln_b, ffn1_up, ffn1_down, ffn2_up, ffn2_down, fox_w_in, fox_b_f, fox_w_out, hgrn_w_in, hgrn_lb, hgrn_norm_g, hgrn_w_out):
    bp, tp, d = x_prompt.shape
    bs, ts, _ = x_sample.shape
    depth = ln_g.shape[0]
    heads = fox_b_f.shape[1]
    n_fox = fox_w_in.shape[0]
    past = cache_fox_k.shape[2]
    assert d == heads * LANES and hgrn_norm_g.shape[1] == LANES
    alpha = (2 * depth) ** 0.25

    tf = 512 if ffn1_down.shape[1] >= 512 else LANES
    up1, down1 = _prep_ffn(ffn1_up, ffn1_down, tf)
    up2, down2 = _prep_ffn(ffn2_up, ffn2_down, tf)
    fox_qkv = fox_w_in[..., :3 * d].astype(BF16)
    fox_wf = jnp.pad(fox_w_in[..., 3 * d:], ((0, 0), (0, 0), (0, LANES - heads))).astype(BF16)
    fox_bf = jnp.pad(fox_b_f, ((0, 0), (0, LANES - heads)))[:, None, :]
    fox_out = fox_w_out.astype(BF16)
    hgrn_in = hgrn_w_in.astype(BF16)
    hgrn_out = hgrn_w_out.astype(BF16)
    hgrn_ng = hgrn_norm_g[:, None, :]
    g3 = ln_g.reshape(depth * 3, 1, d)
    b3 = ln_b.reshape(depth * 3, 1, d)

    xp = x_prompt.reshape(bp * tp, d)
    xs = x_sample.reshape(bs * ts, d)
    kv_p, kv_s = None, None
    logf_p, logf_s, hg_p, hg_s = [], [], [], []

    for i in range(depth):
        j = i // 2
        xp, xpb = _ffn_ln(xp, up1, down1, g3, b3, i, 3 * i, alpha, tf, with_bf16=True)
        xs, xsb = _ffn_ln(xs, up1, down1, g3, b3, i, 3 * i, alpha, tf, with_bf16=True)
        if i % 2 == 0:
            q, k_st, v_st, lf = _fox_proj(xpb, fox_qkv, fox_wf, fox_bf, j, n_fox, kv_p)
            kv_p = (k_st, v_st)
            logf = lf[:, :heads].reshape(bp, tp, heads)
            c_row = _cumsum_heads(logf)
            c_col = jnp.transpose(c_row.reshape(bp, heads, tp), (0, 2, 1)).reshape(bp * tp, heads)
            op = _fox_attn(q, k_st, v_st, j, c_row, c_col, bp, tp, heads)
            logf_p.append(logf)
            q, k_st, v_st, lf = _fox_proj(xsb, fox_qkv, fox_wf, fox_bf, j, n_fox, kv_s)
            kv_s = (k_st, v_st)
            logf = lf[:, :heads].reshape(bs, ts, heads)
            c_all = _cumsum_heads(jnp.concatenate([cache_fox_logf[j].astype(F32), logf], axis=1))
            c_new = c_all[:, past:]
            c_col = jnp.transpose(c_new.reshape(bs, heads, ts), (0, 2, 1)).reshape(bs * ts, heads)
            os_ = _fox_attn_sample(q, k_st, v_st, j, cache_fox_k[j].reshape(bs * past, d),
                                   cache_fox_v[j].reshape(bs * past, d),
                                   c_all[:, :past], c_new, c_col, bs, ts, past, heads)
            logf_s.append(logf)
            w_out = fox_out
        else:
            pp = _proj(xpb, hgrn_in, j, 4 * d)
            op, sp = _hgrn(pp, hgrn_lb, hgrn_ng, None, bp, tp, heads, i, j)
            ps = _proj(xsb, hgrn_in, j, 4 * d)
            os_, ss = _hgrn(ps, hgrn_lb, hgrn_ng, state_hgrn[j], bs, ts, heads, i, j)
            hg_p.append(sp)
            hg_s.append(ss)
            w_out = hgrn_out
        xp = _outproj_ln(op, w_out, xp, g3, b3, j, 3 * i + 1, alpha)
        xs = _outproj_ln(os_, w_out, xs, g3, b3, j, 3 * i + 1, alpha)
        xp = _ffn_ln(xp, up2, down2, g3, b3, i, 3 * i + 2, alpha, tf)
        xs = _ffn_ln(xs, up2, down2, g3, b3, i, 3 * i + 2, alpha, tf)

    return (xp.reshape(bp, tp, d), xs.reshape(bs, ts, d),
            kv_p[0].reshape(n_fox, bp, tp, heads, LANES), kv_p[1].reshape(n_fox, bp, tp, heads, LANES),
            jnp.stack(logf_p), jnp.stack(hg_p),
            kv_s[0].reshape(n_fox, bs, ts, heads, LANES), kv_s[1].reshape(n_fox, bs, ts, heads, LANES),
            jnp.stack(logf_s), jnp.stack(hg_s))
```

```python
import functools
import math

import jax
import jax.numpy as jnp
from jax import lax
from jax.experimental import pallas as pl
from jax.experimental.pallas import tpu as pltpu

F32 = jnp.float32
BF16 = jnp.bfloat16

LN_EPS = 1e-5
RMS_EPS = 1e-6
NEG_INF = -1e30

LANES = 128
HGRN_SUB = 16
HGRN_CHUNK = 128
VMEM_LIMIT_BYTES = 48 * 1024 * 1024


def _round_up(n, m):
    return (n + m - 1) // m * m


def _pick_tile(n, candidates):
    for c in candidates:
        if n % c == 0:
            return c
    return n


def _cparams(*semantics):
    return pltpu.CompilerParams(dimension_semantics=semantics,
                                vmem_limit_bytes=VMEM_LIMIT_BYTES)


def _dot(a, b):
    return jnp.dot(a, b, preferred_element_type=F32)


def _dot_nt(a, b):
    return lax.dot_general(a, b, (((1,), (1,)), ((), ())), preferred_element_type=F32)


def _dot_tn(a, b):
    return lax.dot_general(a, b, (((0,), (0,)), ((), ())), preferred_element_type=F32)


def _sigmoid(x):
    return 1.0 / (1.0 + jnp.exp(-x))


def _log_sigmoid(x):
    return jnp.minimum(x, 0.0) - jnp.log(1.0 + jnp.exp(-jnp.abs(x)))


def _layer_norm(r, g, b):
    mu = jnp.mean(r, axis=-1, keepdims=True)
    c = r - mu
    var = jnp.mean(c * c, axis=-1, keepdims=True)
    return c * lax.rsqrt(var + LN_EPS) * g + b


def _ffn_ln_kernel(x_ref, wg_ref, wu_ref, wd_ref, g_ref, b_ref, *rest, alpha):
    out_refs, (xb_ref, acc_ref) = rest[:-2], rest[-2:]
    f = pl.program_id(1)

    @pl.when(f == 0)
    def _():
        xb_ref[...] = x_ref[...].astype(BF16)
        acc_ref[...] = jnp.zeros_like(acc_ref)

    xb = xb_ref[...]
    gate = _dot(xb, wg_ref[...])
    up = _dot(xb, wu_ref[...])
    h = gate * _sigmoid(gate) * up
    acc_ref[...] += _dot(h.astype(BF16), wd_ref[...])

    @pl.when(f == pl.num_programs(1) - 1)
    def _():
        r = alpha * x_ref[...] + 0.5 * acc_ref[...]
        y = _layer_norm(r, g_ref[...], b_ref[...])
        for o_ref in out_refs:
            o_ref[...] = y.astype(o_ref.dtype)


def _ffn_ln(x, w_up, w_down, ln_g, ln_b, layer, ln_idx, alpha, tf, with_bf16=False):
    m, d = x.shape
    fp = w_down.shape[1]
    nf = fp // tf
    tm = _pick_tile(m, (512, 256, 128, 64, 32, 16, 8))
    out_dtypes = (F32, BF16) if with_bf16 else (F32,)
    outs = pl.pallas_call(
        functools.partial(_ffn_ln_kernel, alpha=alpha),
        grid=(m // tm, nf),
        in_specs=[
            pl.BlockSpec((tm, d), lambda i, f: (i, 0)),
            pl.BlockSpec((None, d, tf), lambda i, f: (layer, 0, f)),
            pl.BlockSpec((None, d, tf), lambda i, f: (layer, 0, f + nf)),
            pl.BlockSpec((None, tf, d), lambda i, f: (layer, f, 0)),
            pl.BlockSpec((None, 1, d), lambda i, f: (ln_idx, 0, 0)),
            pl.BlockSpec((None, 1, d), lambda i, f: (ln_idx, 0, 0)),
        ],
        out_specs=[pl.BlockSpec((tm, d), lambda i, f: (i, 0)) for _ in out_dtypes],
        out_shape=[jax.ShapeDtypeStruct((m, d), dt) for dt in out_dtypes],
        scratch_shapes=[pltpu.VMEM((tm, d), BF16), pltpu.VMEM((tm, d), F32)],
        compiler_params=_cparams("parallel", "arbitrary"),
        name="ffn_ln",
    )(x, w_up, w_up, w_down, ln_g, ln_b)
    return outs if with_bf16 else outs[0]


def _proj_kernel(x_ref, w_ref, o_ref):
    o_ref[...] = _dot(x_ref[...], w_ref[...]).astype(o_ref.dtype)


def _proj(xb, w, layer, n, out_dtype=F32):
    m, d = xb.shape
    tm = _pick_tile(m, (2048, 1024, 512, 256, 128, 64, 32, 16))
    tn = _pick_tile(n, (512, 256, 128))
    return pl.pallas_call(
        _proj_kernel,
        grid=(m // tm, n // tn),
        in_specs=[
            pl.BlockSpec((tm, d), lambda i, j: (i, 0)),
            pl.BlockSpec((None, d, tn), lambda i, j: (layer, 0, j)),
        ],
        out_specs=pl.BlockSpec((tm, tn), lambda i, j: (i, j)),
        out_shape=jax.ShapeDtypeStruct((m, n), out_dtype),
        compiler_params=_cparams("parallel", "arbitrary"),
        name="proj",
    )(xb, w)


def _fox_proj_kernel(x_ref, w_ref, wf_ref, bf_ref, *rest, nq, heads):
    q_ref, kb_ref, vb_ref, k_ref, v_ref, lf_ref = rest[-6:]
    j = pl.program_id(1)
    tm, tn = q_ref.shape
    heads_per_tile = tn // LANES

    def cache_rows(o_ref, y, c):
        for hh in range(heads_per_tile):
            o_ref[pl.ds(c * heads_per_tile + hh, tm, stride=heads), :] = y[:, hh * LANES:(hh + 1) * LANES]

    @pl.when(j < nq)
    def _():
        q_ref[...] = _dot(x_ref[...], w_ref[...]).astype(q_ref.dtype)

    @pl.when((j >= nq) & (j < 2 * nq))
    def _():
        y = _dot(x_ref[...], w_ref[...])
        kb_ref[...] = y.astype(kb_ref.dtype)
        cache_rows(k_ref, y, j - nq)

    @pl.when((j >= 2 * nq) & (j < 3 * nq))
    def _():
        y = _dot(x_ref[...], w_ref[...])
        vb_ref[...] = y.astype(vb_ref.dtype)
        cache_rows(v_ref, y, j - 2 * nq)

    @pl.when(j == 3 * nq)
    def _():
        lf_ref[...] = _log_sigmoid(_dot(x_ref[...], wf_ref[...]) + bf_ref[...])


def _fox_proj(xb, w_qkv, wf, bf, layer, n_layers, heads, kv_stacks):
    m, d = xb.shape
    tm = _pick_tile(m, (512, 256, 128, 64, 32, 16))
    tn = _pick_tile(d, (512, 256, 128))
    nq = d // tn
    in_specs = [
        pl.BlockSpec((tm, d), lambda i, j: (i, 0)),
        pl.BlockSpec((None, d, tn), lambda i, j: (layer, 0, jnp.minimum(j, 3 * nq - 1))),
        pl.BlockSpec((None, d, LANES), lambda i, j: (layer, 0, 0)),
        pl.BlockSpec((None, 1, LANES), lambda i, j: (layer, 0, 0)),
    ]
    args = [xb, w_qkv, wf, bf]
    aliases = {}
    if kv_stacks is not None:
        in_specs += [pl.BlockSpec(memory_space=pl.ANY), pl.BlockSpec(memory_space=pl.ANY)]
        args += list(kv_stacks)
        aliases = {4: 3, 5: 4}
    return pl.pallas_call(
        functools.partial(_fox_proj_kernel, nq=nq, heads=heads),
        grid=(m // tm, 3 * nq + 1),
        in_specs=in_specs,
        out_specs=[
            pl.BlockSpec((tm, tn), lambda i, j: (i, jnp.minimum(j, nq - 1))),
            pl.BlockSpec((tm, tn), lambda i, j: (i, jnp.clip(j - nq, 0, nq - 1))),
            pl.BlockSpec((tm, tn), lambda i, j: (i, jnp.clip(j - 2 * nq, 0, nq - 1))),
            pl.BlockSpec((None, tm * heads, LANES), lambda i, j: (layer, i, 0)),
            pl.BlockSpec((None, tm * heads, LANES), lambda i, j: (layer, i, 0)),
            pl.BlockSpec((tm, LANES), lambda i, j: (i, 0)),
        ],
        out_shape=[
            jax.ShapeDtypeStruct((m, d), BF16),
            jax.ShapeDtypeStruct((m, d), BF16),
            jax.ShapeDtypeStruct((m, d), BF16),
            jax.ShapeDtypeStruct((n_layers, m * heads, LANES), F32),
            jax.ShapeDtypeStruct((n_layers, m * heads, LANES), F32),
            jax.ShapeDtypeStruct((m, LANES), F32),
        ],
        input_output_aliases=aliases,
        compiler_params=_cparams("parallel", "arbitrary"),
        name="fox_proj",
    )(*args)


def _cumsum_lanes_kernel(x_ref, o_ref):
    x = x_ref[...]
    n = x.shape[-1]
    lane = lax.broadcasted_iota(jnp.int32, x.shape, 1)
    s = 1
    while s < n:
        x = x + jnp.where(lane >= s, pltpu.roll(x, s, 1), 0.0)
        s *= 2
    o_ref[...] = x


def _cumsum_lanes(x):
    r, n = x.shape
    rb = _pick_tile(r, (64, 32, 16, 8))
    return pl.pallas_call(
        _cumsum_lanes_kernel,
        grid=(r // rb,),
        in_specs=[pl.BlockSpec((rb, n), lambda i: (i, 0))],
        out_specs=pl.BlockSpec((rb, n), lambda i: (i, 0)),
        out_shape=jax.ShapeDtypeStruct((r, n), F32),
        compiler_params=_cparams("parallel"),
        name="cumsum_lanes",
    )(x)


def _head_column(c, h):
    lane = lax.broadcasted_iota(jnp.int32, c.shape, 1)
    return jnp.sum(jnp.where(lane == h, c, 0.0), axis=1, keepdims=True)


_LOG2E = math.log2(math.e)


def _softmax_tile(u, ci, m, l, acc, v, a_coef):
    m_new = jnp.maximum(m, jnp.max(u, axis=1, keepdims=True) + ci)
    alpha = jnp.exp2(a_coef * (m - m_new))
    p = jnp.exp2(a_coef * (u - (m_new - ci)))
    l = alpha * l + jnp.sum(p, axis=1, keepdims=True)
    acc = alpha * acc + _dot(p.astype(BF16), v)
    return m_new, l, acc


def _fox_attn_kernel(q_ref, kb_ref, vb_ref, crow_ref, ccol_ref, o_ref, *, tq, scale, hp):
    g = pl.program_id(1)
    a_coef = scale * _LOG2E
    inv_scale = 1.0 / scale
    seq = q_ref.shape[0]
    cc = ccol_ref[...]
    row = lax.broadcasted_iota(jnp.int32, (tq, tq), 0)
    col = lax.broadcasted_iota(jnp.int32, (tq, tq), 1)
    ci_all = [_head_column(cc, g * hp + hh) * inv_scale for hh in range(hp)]
    cj_all = [crow_ref[hh] * inv_scale for hh in range(hp)]
    starts = list(range(0, seq, tq))
    pairs = [starts[i:i + 1] + starts[len(starts) - 1 - i:len(starts) - i]
             for i in range(len(starts) // 2)] or [starts]
    for pair in pairs:
        items = [(hh, r0) for r0 in pair for hh in range(hp)]
        sls = {hh: slice(hh * LANES, (hh + 1) * LANES) for hh in range(hp)}
        u = {}
        for (hh, r0) in items:
            w = r0 + tq
            u[hh, r0] = _dot_nt(q_ref[r0:w, sls[hh]], kb_ref[0:w, sls[hh]]) - cj_all[hh][:, 0:w]
        r = {}
        for (hh, r0) in items:
            w = r0 + tq
            diag = jnp.where(col <= row, u[hh, r0][:, r0:w], NEG_INF)
            u[hh, r0] = diag if r0 == 0 else jnp.concatenate([u[hh, r0][:, 0:r0], diag], axis=1)
            ci = ci_all[hh][r0:w, :]
            m = jnp.max(u[hh, r0], axis=1, keepdims=True) + ci
            r[hh, r0] = m - ci
        p, l = {}, {}
        for t in items:
            pt = jnp.exp2(a_coef * (u[t] - r[t]))
            l[t] = jnp.sum(pt, axis=1, keepdims=True)
            p[t] = pt.astype(BF16)
        for (hh, r0) in items:
            w = r0 + tq
            o = _dot(p[hh, r0], vb_ref[0:w, sls[hh]]) / l[hh, r0]
            o_ref[r0:w, sls[hh]] = o.astype(o_ref.dtype)


def _fox_attn(q, kb, vb, c_row, c_col, batch, seq, heads):
    m, d = q.shape
    hp = 2 if heads % 2 == 0 else 1
    tq = _pick_tile(seq, (256, 128))
    ng = heads // hp
    w = hp * LANES
    c_row = c_row.reshape(batch * ng, hp, 1, seq)
    return pl.pallas_call(
        functools.partial(_fox_attn_kernel, tq=tq, scale=LANES ** -0.5, hp=hp),
        grid=(batch, ng),
        in_specs=[
            pl.BlockSpec((seq, w), lambda b, g: (b, g)),
            pl.BlockSpec((seq, w), lambda b, g: (b, g)),
            pl.BlockSpec((seq, w), lambda b, g: (b, g)),
            pl.BlockSpec((None, hp, 1, seq), lambda b, g: (b * ng + g, 0, 0, 0)),
            pl.BlockSpec((seq, heads), lambda b, g: (b, 0)),
        ],
        out_specs=pl.BlockSpec((seq, w), lambda b, g: (b, g)),
        out_shape=jax.ShapeDtypeStruct((m, d), BF16),
        compiler_params=_cparams("parallel", "parallel"),
        name="fox_attn",
    )(q, kb, vb, c_row, c_col)


def _fox_attn_sample_kernel(q_ref, kn_ref, vn_ref, kp_ref, vp_ref, cpast_ref, cnew_ref, ccol_ref,
                            o_ref, *, scale, hp):
    g = pl.program_id(1)
    a_coef = scale * _LOG2E
    inv_scale = 1.0 / scale
    ts = q_ref.shape[0]
    cc = ccol_ref[...]
    row = lax.broadcasted_iota(jnp.int32, (ts, ts), 0)
    col = lax.broadcasted_iota(jnp.int32, (ts, ts), 1)
    for hh in range(hp):
        sl = slice(hh * LANES, (hh + 1) * LANES)
        q = q_ref[:, sl]
        ci = _head_column(cc, g * hp + hh) * inv_scale
        carry = (jnp.full((ts, 1), NEG_INF, F32), jnp.zeros((ts, 1), F32), jnp.zeros((ts, LANES), F32))
        up = _dot_nt(q, kp_ref[:, sl].astype(BF16)) - cpast_ref[hh] * inv_scale
        carry = _softmax_tile(up, ci, *carry, vp_ref[:, sl].astype(BF16), a_coef)
        un = _dot_nt(q, kn_ref[:, sl]) - cnew_ref[hh] * inv_scale
        un = jnp.where(col <= row, un, NEG_INF)
        _, l, acc = _softmax_tile(un, ci, *carry, vn_ref[:, sl], a_coef)
        o_ref[:, sl] = (acc / l).astype(o_ref.dtype)


def _fox_attn_sample(q, kb, vb, k_past, v_past, c_past, c_new, c_col, batch, ts, past, heads):
    m, d = q.shape
    hp = _pick_tile(heads, (8, 4, 2, 1))
    ng = heads // hp
    w = hp * LANES
    c_past = c_past.reshape(batch * ng, hp, 1, past)
    c_new = c_new.reshape(batch * ng, hp, 1, ts)
    return pl.pallas_call(
        functools.partial(_fox_attn_sample_kernel, scale=LANES ** -0.5, hp=hp),
        grid=(batch, ng),
        in_specs=[
            pl.BlockSpec((ts, w), lambda b, g: (b, g)),
            pl.BlockSpec((ts, w), lambda b, g: (b, g)),
            pl.BlockSpec((ts, w), lambda b, g: (b, g)),
            pl.BlockSpec((past, w), lambda b, g: (b, g)),
            pl.BlockSpec((past, w), lambda b, g: (b, g)),
            pl.BlockSpec((None, hp, 1, past), lambda b, g: (b * ng + g, 0, 0, 0)),
            pl.BlockSpec((None, hp, 1, ts), lambda b, g: (b * ng + g, 0, 0, 0)),
            pl.BlockSpec((ts, heads), lambda b, g: (b, 0)),
        ],
        out_specs=pl.BlockSpec((ts, w), lambda b, g: (b, g)),
        out_shape=jax.ShapeDtypeStruct((m, d), BF16),
        compiler_params=_cparams("parallel", "parallel"),
        name="fox_attn_sample",
    )(q, kb, vb, k_past, v_past, c_past, c_new, c_col)


def _expand_rows(f, reps):
    parts = [jnp.broadcast_to(f[r:r + 1, :], (reps, f.shape[1])) for r in range(f.shape[0])]
    return parts[0] if len(parts) == 1 else jnp.concatenate(parts, axis=0)


def _take_rows(f, idx):
    parts = [f[r:r + 1, :] for r in idx]
    return parts[0] if len(parts) == 1 else jnp.concatenate(parts, axis=0)


def _hgrn_group(q, z, v, go, states, bscr_ref, lb, ng, kscale, n):
    n_g = q.shape[0] // n
    hp = q.shape[1] // LANES
    nblk = n // HGRN_SUB
    tiles = [(u, hh) for u in range(n_g) for hh in range(hp)]

    def tile(x, u, hh):
        return x[u * n:(u + 1) * n, hh * LANES:(hh + 1) * LANES]

    e = jnp.exp(-jnp.abs(z))
    r1 = 1.0 / (1.0 + e)
    er1 = e * r1
    pos = z > 0.0
    g = jnp.log(lb + (1.0 - lb) * jnp.where(pos, r1, er1))
    kk = (1.0 - lb) * jnp.where(pos, er1, r1)
    qf = q * _sigmoid(q) * kscale
    gate = go * _sigmoid(go)
    vb = v.astype(BF16)

    row = lax.broadcasted_iota(jnp.int32, (n, n), 0)
    col = lax.broadcasted_iota(jnp.int32, (n, n), 1)
    tri = jnp.where(col <= row, 1.0, 0.0).astype(BF16)
    g_hi = g.astype(BF16)
    g_r1 = g - g_hi.astype(F32)
    g_mid = g_r1.astype(BF16)
    g_lo = (g_r1 - g_mid.astype(F32)).astype(BF16)
    b = {t: _dot(tri, tile(g_hi, *t)) + _dot(tri, tile(g_mid, *t)) + _dot(tri, tile(g_lo, *t))
         for t in tiles}
    for (u, hh) in tiles:
        bscr_ref[hh, u] = b[u, hh]
    bmid = {(u, hh): bscr_ref[hh, u, pl.ds(HGRN_SUB // 2, nblk, stride=HGRN_SUB), :] for (u, hh) in tiles}
    bend = {(u, hh): bscr_ref[hh, u, pl.ds(HGRN_SUB - 1, nblk, stride=HGRN_SUB), :] for (u, hh) in tiles}
    blast = {t: bend[t][nblk - 1:nblk, :] for t in tiles}
    qd, kd = {}, {}
    for t in tiles:
        bmid_rows = _expand_rows(bmid[t], HGRN_SUB)
        qd[t] = tile(qf, *t) * jnp.exp(b[t] - bmid_rows)
        kd[t] = tile(kk, *t) * jnp.exp(bmid_rows - b[t])

    blk = lax.broadcasted_iota(jnp.int32, (nblk, LANES), 0)
    a = {t: None for t in tiles}
    s = nblk // 2
    while s >= 1:
        upper = ((blk >> (s.bit_length() - 1)) & 1) == 1
        group_shift = (2 * s * HGRN_SUB).bit_length() - 1
        same_group = (row >> group_shift) == (col >> group_shift)
        for t in tiles:
            ref = _take_rows(bend[t], [(i // (2 * s)) * 2 * s + s - 1 for i in range(nblk)])
            fq = jnp.where(upper, jnp.exp(jnp.minimum(bmid[t] - ref, 0.0)), 0.0)
            fk = jnp.where(upper, 0.0, jnp.exp(jnp.minimum(ref - bmid[t], 0.0)))
            ql = qd[t] * _expand_rows(fq, HGRN_SUB)
            kl = kd[t] * _expand_rows(fk, HGRN_SUB)
            prod = _dot_nt(ql.astype(BF16), kl.astype(BF16))
            a[t] = prod if a[t] is None else jnp.where(same_group, prod, a[t])
        s //= 2
    sub_shift = HGRN_SUB.bit_length() - 1
    on_diag = ((row >> sub_shift) == (col >> sub_shift)) & (col <= row)
    o_in, q_in, upd, decay = {}, {}, {}, {}
    for t in tiles:
        pd = _dot_nt(qd[t].astype(BF16), kd[t].astype(BF16))
        at = jnp.where(on_diag, pd, 0.0 if a[t] is None else a[t])
        vt = tile(vb, *t)
        o_in[t] = _dot(at.astype(BF16), vt)
        q_in[t] = (qd[t] * _expand_rows(jnp.exp(bmid[t]), HGRN_SUB)).astype(BF16)
        k_out = kd[t] * _expand_rows(jnp.exp(blast[t] - bmid[t]), HGRN_SUB)
        upd[t] = _dot_tn(vt, k_out.astype(BF16))
        decay[t] = jnp.exp(blast[t])

    states = list(states)
    rows_out = []
    for u in range(n_g):
        cols_out = []
        for hh in range(hp):
            t = (u, hh)
            o = o_in[t] + _dot_nt(q_in[t], states[hh].astype(BF16))
            states[hh] = states[hh] * decay[t] + upd[t]
            o = o * lax.rsqrt(jnp.mean(o * o, axis=-1, keepdims=True) + RMS_EPS) * ng
            cols_out.append(o * tile(gate, *t))
        rows_out.append(cols_out[0] if hp == 1 else jnp.concatenate(cols_out, axis=1))
    out = rows_out[0] if n_g == 1 else jnp.concatenate(rows_out, axis=0)
    return out, states


def _hgrn_kernel(*refs, chunk, n_chunks, has_s0, layer, kscale, hp):
    if has_s0:
        q_ref, z_ref, i_ref, g_ref, lbp_ref, ng_ref, s0_ref, o_ref, s_ref, bscr_ref = refs
    else:
        q_ref, z_ref, i_ref, g_ref, lbp_ref, ng_ref, o_ref, s_ref, bscr_ref = refs
        s0_ref = None

    lbp = lbp_ref[...]
    e = jnp.exp(lbp - jnp.max(lbp, axis=0, keepdims=True))
    soft = e / jnp.sum(e, axis=0, keepdims=True)
    r = lax.broadcasted_iota(jnp.int32, lbp.shape, 0)
    lb_all = jnp.sum(jnp.where((r >= 1) & (r <= layer), soft, 0.0), axis=0, keepdims=True)
    ng = ng_ref[...]

    group = bscr_ref.shape[1]

    def step(t, states):
        rows = pl.ds(pl.multiple_of(t * group * chunk, group * chunk), group * chunk)
        o, states = _hgrn_group(q_ref[rows, :], z_ref[rows, :], i_ref[rows, :], g_ref[rows, :],
                                states, bscr_ref, lb_all, ng, kscale, chunk)
        o_ref[rows, :] = o.astype(o_ref.dtype)
        return tuple(states)

    if has_s0:
        init = tuple(s0_ref[hh].T for hh in range(hp))
    else:
        init = tuple(jnp.zeros((LANES, LANES), F32) for _ in range(hp))
    if n_chunks == group:
        states = step(0, init)
    else:
        states = lax.fori_loop(0, n_chunks // group, step, init)
    for hh in range(hp):
        s_ref[hh] = states[hh].T


def _hgrn(proj, lb_param, norm_g, s0, batch, seq, heads, layer, mixer_idx):
    m = proj.shape[0]
    d = heads * LANES
    chunk = _pick_tile(seq, (HGRN_CHUNK, 64, 32, 16))
    assert chunk % HGRN_SUB == 0 and seq % chunk == 0
    depth = lb_param.shape[0]
    has_s0 = s0 is not None
    hp = _pick_tile(heads, (2, 1) if seq > chunk else (4, 2, 1))
    group = _pick_tile(seq // chunk, (4, 2, 1))
    ng = heads // hp
    w = hp * LANES
    in_specs = [
        pl.BlockSpec((seq, w), lambda b, g: (b, g)),
        pl.BlockSpec((seq, w), lambda b, g: (b, ng + g)),
        pl.BlockSpec((seq, w), lambda b, g: (b, 2 * ng + g)),
        pl.BlockSpec((seq, w), lambda b, g: (b, 3 * ng + g)),
        pl.BlockSpec((depth, w), lambda b, g: (0, g)),
        pl.BlockSpec((None, 1, LANES), lambda b, g: (mixer_idx, 0, 0)),
    ]
    args = [proj, proj, proj, proj, lb_param, norm_g]
    if has_s0:
        in_specs.append(pl.BlockSpec((None, hp, LANES, LANES), lambda b, g: (b, g, 0, 0)))
        args.append(s0)
    return pl.pallas_call(
        functools.partial(_hgrn_kernel, chunk=chunk, n_chunks=seq // chunk, has_s0=has_s0,
                          layer=layer, kscale=LANES ** -0.5, hp=hp),
        grid=(batch, ng),
        in_specs=in_specs,
        out_specs=[
            pl.BlockSpec((seq, w), lambda b, g: (b, g)),
            pl.BlockSpec((None, hp, LANES, LANES), lambda b, g: (b, g, 0, 0)),
        ],
        out_shape=[
            jax.ShapeDtypeStruct((m, d), BF16),
            jax.ShapeDtypeStruct((batch, heads, LANES, LANES), F32),
        ],
        scratch_shapes=[pltpu.VMEM((hp, group, chunk, LANES), F32)],
        compiler_params=_cparams("parallel", "parallel"),
        name="hgrn",
    )(*args)


def _outproj_ln_kernel(o_ref, w_ref, x_ref, g_ref, b_ref, y_ref, *, alpha, rows):
    for r0 in range(0, o_ref.shape[0], rows):
        sl = slice(r0, r0 + rows)
        r = alpha * x_ref[sl, :] + _dot(o_ref[sl, :], w_ref[...])
        y_ref[sl, :] = _layer_norm(r, g_ref[...], b_ref[...])


def _outproj_ln(o, w, x, ln_g, ln_b, layer, ln_idx, alpha):
    m, d = x.shape
    tm = _pick_tile(m, (512, 256, 128, 64, 32, 16))
    return pl.pallas_call(
        functools.partial(_outproj_ln_kernel, alpha=alpha, rows=_pick_tile(tm, (128,))),
        grid=(m // tm,),
        in_specs=[
            pl.BlockSpec((tm, d), lambda i: (i, 0)),
            pl.BlockSpec((None, d, d), lambda i: (layer, 0, 0)),
            pl.BlockSpec((tm, d), lambda i: (i, 0)),
            pl.BlockSpec((None, 1, d), lambda i: (ln_idx, 0, 0)),
            pl.BlockSpec((None, 1, d), lambda i: (ln_idx, 0, 0)),
        ],
        out_specs=pl.BlockSpec((tm, d), lambda i: (i, 0)),
        out_shape=jax.ShapeDtypeStruct((m, d), F32),
        compiler_params=_cparams("parallel"),
        name="outproj_ln",
    )(o, w, x, ln_g, ln_b)


def _prep_ffn(w_up, w_down, tf):
    f = w_down.shape[1]
    fp = _round_up(f, tf)
    pad = ((0, 0), (0, 0), (0, fp - f))
    up = jnp.concatenate([jnp.pad(w_up[..., :f], pad), jnp.pad(w_up[..., f:], pad)], axis=-1)
    down = jnp.pad(w_down, ((0, 0), (0, fp - f), (0, 0)))
    return up.astype(BF16), down.astype(BF16)


def _cumsum_heads(logf_bsh):
    b, s, h = logf_bsh.shape
    sp = _round_up(s, LANES)
    x = jnp.transpose(logf_bsh, (0, 2, 1)).reshape(b * h, s)
    x = jnp.pad(x, ((0, 0), (0, sp - s)))
    return _cumsum_lanes(x)[:, :s]


def kernel(x_prompt, x_sample, cache_fox_k, cache_fox_v, cache_fox_logf, state_hgrn, ln_g, ln_b, ffn1_up, ffn1_down, ffn2_up, ffn2_down, fox_w_in, fox_b_f, fox_w_out, hgrn_w_in, hgrn_lb, hgrn_norm_g, hgrn_w_out):
    bp, tp, d = x_prompt.shape
    bs, ts, _ = x_sample.shape
    depth = ln_g.shape[0]
    heads = fox_b_f.shape[1]
    n_fox = fox_w_in.shape[0]
    past = cache_fox_k.shape[2]
    assert d == heads * LANES and hgrn_norm_g.shape[1] == LANES
    alpha = (2 * depth) ** 0.25

    tf = 512 if ffn1_down.shape[1] >= 512 else LANES
    up1, down1 = _prep_ffn(ffn1_up, ffn1_down, tf)
    up2, down2 = _prep_ffn(ffn2_up, ffn2_down, tf)
    fox_qkv = fox_w_in[..., :3 * d].astype(BF16)
    fox_wf = jnp.pad(fox_w_in[..., 3 * d:], ((0, 0), (0, 0), (0, LANES - heads))).astype(BF16)
    fox_bf = jnp.pad(fox_b_f, ((0, 0), (0, LANES - heads)))[:, None, :]
    fox_out = fox_w_out.astype(BF16)
    hgrn_in = hgrn_w_in.astype(BF16)
    hgrn_out = hgrn_w_out.astype(BF16)
    hgrn_ng = hgrn_norm_g[:, None, :]
    g3 = ln_g.reshape(depth * 3, 1, d)
    b3 = ln_b.reshape(depth * 3, 1, d)

    xp = x_prompt.reshape(bp * tp, d)
    xs = x_sample.reshape(bs * ts, d)
    kv_p, kv_s = None, None
    logf_p, logf_s, hg_p, hg_s = [], [], [], []

    for i in range(depth):
        j = i // 2
        xp, xpb = _ffn_ln(xp, up1, down1, g3, b3, i, 3 * i, alpha, tf, with_bf16=True)
        xs, xsb = _ffn_ln(xs, up1, down1, g3, b3, i, 3 * i, alpha, tf, with_bf16=True)
        if i % 2 == 0:
            q, kb, vb, k_st, v_st, lf = _fox_proj(xpb, fox_qkv, fox_wf, fox_bf, j, n_fox, heads, kv_p)
            kv_p = (k_st, v_st)
            logf = lf[:, :heads].reshape(bp, tp, heads)
            c_row = _cumsum_heads(logf)
            c_col = jnp.transpose(c_row.reshape(bp, heads, tp), (0, 2, 1)).reshape(bp * tp, heads)
            op = _fox_attn(q, kb, vb, c_row, c_col, bp, tp, heads)
            logf_p.append(logf)
            q, kb, vb, k_st, v_st, lf = _fox_proj(xsb, fox_qkv, fox_wf, fox_bf, j, n_fox, heads, kv_s)
            kv_s = (k_st, v_st)
            logf = lf[:, :heads].reshape(bs, ts, heads)
            c_all = _cumsum_heads(jnp.concatenate([cache_fox_logf[j].astype(F32), logf], axis=1))
            c_new = c_all[:, past:]
            c_col = jnp.transpose(c_new.reshape(bs, heads, ts), (0, 2, 1)).reshape(bs * ts, heads)
            os_ = _fox_attn_sample(q, kb, vb, cache_fox_k[j].reshape(bs * past, d),
                                   cache_fox_v[j].reshape(bs * past, d),
                                   c_all[:, :past], c_new, c_col, bs, ts, past, heads)
            logf_s.append(logf)
            w_out = fox_out
        else:
            pp = _proj(xpb, hgrn_in, j, 4 * d)
            op, sp = _hgrn(pp, hgrn_lb, hgrn_ng, None, bp, tp, heads, i, j)
            ps = _proj(xsb, hgrn_in, j, 4 * d)
            os_, ss = _hgrn(ps, hgrn_lb, hgrn_ng, state_hgrn[j], bs, ts, heads, i, j)
            hg_p.append(sp)
            hg_s.append(ss)
            w_out = hgrn_out
        xp = _outproj_ln(op, w_out, xp, g3, b3, j, 3 * i + 1, alpha)
        xs = _outproj_ln(os_, w_out, xs, g3, b3, j, 3 * i + 1, alpha)
        xp = _ffn_ln(xp, up2, down2, g3, b3, i, 3 * i + 2, alpha, tf)
        xs = _ffn_ln(xs, up2, down2, g3, b3, i, 3 * i + 2, alpha, tf)

    return (xp.reshape(bp, tp, d), xs.reshape(bs, ts, d),
            kv_p[0].reshape(n_fox, bp, tp, heads, LANES), kv_p[1].reshape(n_fox, bp, tp, heads, LANES),
            jnp.stack(logf_p), jnp.stack(hg_p),
            kv_s[0].reshape(n_fox, bs, ts, heads, LANES), kv_s[1].reshape(n_fox, bs, ts, heads, LANES),
            jnp.stack(logf_s), jnp.stack(hg_s))
```

```python
import functools
import math

import jax
import jax.numpy as jnp
from jax import lax
from jax.experimental import pallas as pl
from jax.experimental.pallas import tpu as pltpu

F32 = jnp.float32
BF16 = jnp.bfloat16

LN_EPS = 1e-5
RMS_EPS = 1e-6
NEG_INF = -1e30

LANES = 128
HGRN_SUB = 16
HGRN_CHUNK = 128
VMEM_LIMIT_BYTES = 48 * 1024 * 1024


def _round_up(n, m):
    return (n + m - 1) // m * m


def _pick_tile(n, candidates):
    for c in candidates:
        if n % c == 0:
            return c
    return n


def _cparams(*semantics):
    return pltpu.CompilerParams(dimension_semantics=semantics,
                                vmem_limit_bytes=VMEM_LIMIT_BYTES)


def _dot(a, b):
    return jnp.dot(a, b, preferred_element_type=F32)


def _dot_nt(a, b):
    return lax.dot_general(a, b, (((1,), (1,)), ((), ())), preferred_element_type=F32)


def _dot_tn(a, b):
    return lax.dot_general(a, b, (((0,), (0,)), ((), ())), preferred_element_type=F32)


def _sigmoid(x):
    return 1.0 / (1.0 + jnp.exp(-x))


def _log_sigmoid(x):
    return jnp.minimum(x, 0.0) - jnp.log(1.0 + jnp.exp(-jnp.abs(x)))


def _layer_norm(r, g, b):
    mu = jnp.mean(r, axis=-1, keepdims=True)
    c = r - mu
    var = jnp.mean(c * c, axis=-1, keepdims=True)
    return c * lax.rsqrt(var + LN_EPS) * g + b


def _ffn_ln_kernel(x_ref, wg_ref, wu_ref, wd_ref, g_ref, b_ref, *rest, alpha, n_tiles, ln_rows):
    out_refs, (xb_ref, acc_ref) = rest[:-2], rest[-2:]
    i = pl.program_id(0)
    f = pl.program_id(1)
    slot = lax.rem(i, 2)
    tm = xb_ref.shape[0]

    def norm_rows(src_slot):
        start = pl.multiple_of(jnp.minimum(f * ln_rows, tm - ln_rows), 16)
        rows = pl.ds(start, ln_rows)
        y = _layer_norm(0.5 * acc_ref[src_slot, rows, :], g_ref[...], b_ref[...])
        for o_ref in out_refs:
            o_ref[rows, :] = y.astype(o_ref.dtype)

    @pl.when((i == 0) & (f == 0))
    def _():
        acc_ref[1] = jnp.zeros(acc_ref.shape[1:], F32)

    @pl.when(i < n_tiles)
    def _():
        @pl.when(f == 0)
        def _():
            x = x_ref[...]
            xb_ref[...] = x.astype(BF16)
            acc_ref[slot] = (2.0 * alpha) * x

        norm_rows(1 - slot)
        xb = xb_ref[...]
        gate = _dot(xb, wg_ref[...])
        up = _dot(xb, wu_ref[...])
        h = gate * _sigmoid(gate) * up
        acc_ref[slot] += _dot(h.astype(BF16), wd_ref[...])

    @pl.when(i == n_tiles)
    def _():
        norm_rows(1 - slot)


def _ffn_ln(x, weights, ln_g, ln_b, layer, ln_idx, alpha, tf, with_bf16=False):
    w_gate, w_up, w_down = weights
    m, d = x.shape
    fp = w_down.shape[1]
    nf = fp // tf
    tm = _pick_tile(m, (512, 256, 128, 64, 32, 16))
    nt = m // tm
    ln_rows = min(tm, _round_up(-(-tm // nf), 16))
    out_dtypes = (F32, BF16) if with_bf16 else (F32,)

    def w_step(i, f):
        return jnp.where(i < nt, f, nf - 1)

    outs = pl.pallas_call(
        functools.partial(_ffn_ln_kernel, alpha=alpha, n_tiles=nt, ln_rows=ln_rows),
        grid=(nt + 1, nf),
        in_specs=[
            pl.BlockSpec((tm, d), lambda i, f: (jnp.minimum(i, nt - 1), 0)),
            pl.BlockSpec((None, d, tf), lambda i, f: (layer, 0, w_step(i, f))),
            pl.BlockSpec((None, d, tf), lambda i, f: (layer, 0, w_step(i, f))),
            pl.BlockSpec((None, tf, d), lambda i, f: (layer, w_step(i, f), 0)),
            pl.BlockSpec((None, 1, d), lambda i, f: (ln_idx, 0, 0)),
            pl.BlockSpec((None, 1, d), lambda i, f: (ln_idx, 0, 0)),
        ],
        out_specs=[pl.BlockSpec((tm, d), lambda i, f: (jnp.maximum(i - 1, 0), 0)) for _ in out_dtypes],
        out_shape=[jax.ShapeDtypeStruct((m, d), dt) for dt in out_dtypes],
        scratch_shapes=[pltpu.VMEM((tm, d), BF16), pltpu.VMEM((2, tm, d), F32)],
        compiler_params=_cparams("arbitrary", "arbitrary"),
        name="ffn_ln",
    )(x, w_gate, w_up, w_down, ln_g, ln_b)
    return outs if with_bf16 else outs[0]


def _proj_kernel(x_ref, w_ref, o_ref):
    o_ref[...] = _dot(x_ref[...], w_ref[...]).astype(o_ref.dtype)


def _proj(xb, w, layer, n, out_dtype=F32):
    m, d = xb.shape
    tm = _pick_tile(m, (2048, 1024, 512, 256, 128, 64, 32, 16))
    tn = _pick_tile(n, (512, 256, 128))
    return pl.pallas_call(
        _proj_kernel,
        grid=(m // tm, n // tn),
        in_specs=[
            pl.BlockSpec((tm, d), lambda i, j: (i, 0)),
            pl.BlockSpec((None, d, tn), lambda i, j: (layer, 0, j)),
        ],
        out_specs=pl.BlockSpec((tm, tn), lambda i, j: (i, j)),
        out_shape=jax.ShapeDtypeStruct((m, n), out_dtype),
        compiler_params=_cparams("parallel", "arbitrary"),
        name="proj",
    )(xb, w)


def _fox_proj_kernel(x_ref, w_ref, wf_ref, bf_ref, *rest, nq, heads):
    q_ref, kb_ref, vb_ref, k_ref, v_ref, lf_ref = rest[-6:]
    j = pl.program_id(1)
    tm, tn = q_ref.shape
    heads_per_tile = tn // LANES

    def cache_rows(o_ref, y, c):
        for hh in range(heads_per_tile):
            o_ref[pl.ds(c * heads_per_tile + hh, tm, stride=heads), :] = y[:, hh * LANES:(hh + 1) * LANES]

    @pl.when(j < nq)
    def _():
        q_ref[...] = _dot(x_ref[...], w_ref[...]).astype(q_ref.dtype)

    @pl.when((j >= nq) & (j < 2 * nq))
    def _():
        y = _dot(x_ref[...], w_ref[...])
        kb_ref[...] = y.astype(kb_ref.dtype)
        cache_rows(k_ref, y, j - nq)

    @pl.when((j >= 2 * nq) & (j < 3 * nq))
    def _():
        y = _dot(x_ref[...], w_ref[...])
        vb_ref[...] = y.astype(vb_ref.dtype)
        cache_rows(v_ref, y, j - 2 * nq)

    @pl.when(j == 3 * nq)
    def _():
        lf_ref[...] = _log_sigmoid(_dot(x_ref[...], wf_ref[...]) + bf_ref[...])


def _fox_proj(xb, w_qkv, wf, bf, layer, n_layers, heads, kv_stacks):
    m, d = xb.shape
    tm = _pick_tile(m, (512, 256, 128, 64, 32, 16))
    tn = _pick_tile(d, (1024, 512, 256, 128))
    nq = d // tn
    in_specs = [
        pl.BlockSpec((tm, d), lambda i, j: (i, 0)),
        pl.BlockSpec((None, d, tn), lambda i, j: (layer, 0, jnp.minimum(j, 3 * nq - 1))),
        pl.BlockSpec((None, d, LANES), lambda i, j: (layer, 0, 0)),
        pl.BlockSpec((None, 1, LANES), lambda i, j: (layer, 0, 0)),
    ]
    args = [xb, w_qkv, wf, bf]
    aliases = {}
    if kv_stacks is not None:
        in_specs += [pl.BlockSpec(memory_space=pl.ANY), pl.BlockSpec(memory_space=pl.ANY)]
        args += list(kv_stacks)
        aliases = {4: 3, 5: 4}
    return pl.pallas_call(
        functools.partial(_fox_proj_kernel, nq=nq, heads=heads),
        grid=(m // tm, 3 * nq + 1),
        in_specs=in_specs,
        out_specs=[
            pl.BlockSpec((tm, tn), lambda i, j: (i, jnp.minimum(j, nq - 1))),
            pl.BlockSpec((tm, tn), lambda i, j: (i, jnp.clip(j - nq, 0, nq - 1))),
            pl.BlockSpec((tm, tn), lambda i, j: (i, jnp.clip(j - 2 * nq, 0, nq - 1))),
            pl.BlockSpec((None, tm * heads, LANES), lambda i, j: (layer, i, 0)),
            pl.BlockSpec((None, tm * heads, LANES), lambda i, j: (layer, i, 0)),
            pl.BlockSpec((tm, LANES), lambda i, j: (i, 0)),
        ],
        out_shape=[
            jax.ShapeDtypeStruct((m, d), BF16),
            jax.ShapeDtypeStruct((m, d), BF16),
            jax.ShapeDtypeStruct((m, d), BF16),
            jax.ShapeDtypeStruct((n_layers, m * heads, LANES), F32),
            jax.ShapeDtypeStruct((n_layers, m * heads, LANES), F32),
            jax.ShapeDtypeStruct((m, LANES), F32),
        ],
        input_output_aliases=aliases,
        compiler_params=_cparams("parallel", "arbitrary"),
        name="fox_proj",
    )(*args)


def _cumsum_lanes_kernel(x_ref, o_ref):
    x = x_ref[...]
    n = x.shape[-1]
    lane = lax.broadcasted_iota(jnp.int32, x.shape, 1)
    s = 1
    while s < n:
        x = x + jnp.where(lane >= s, pltpu.roll(x, s, 1), 0.0)
        s *= 2
    o_ref[...] = x


def _cumsum_lanes(x):
    r, n = x.shape
    rb = _pick_tile(r, (64, 32, 16, 8))
    return pl.pallas_call(
        _cumsum_lanes_kernel,
        grid=(r // rb,),
        in_specs=[pl.BlockSpec((rb, n), lambda i: (i, 0))],
        out_specs=pl.BlockSpec((rb, n), lambda i: (i, 0)),
        out_shape=jax.ShapeDtypeStruct((r, n), F32),
        compiler_params=_cparams("parallel"),
        name="cumsum_lanes",
    )(x)


def _head_column(c, h):
    lane = lax.broadcasted_iota(jnp.int32, c.shape, 1)
    return jnp.sum(jnp.where(lane == h, c, 0.0), axis=1, keepdims=True)


_LOG2E = math.log2(math.e)


def _fox_attn_kernel(q_ref, kb_ref, vb_ref, crow_ref, ccol_ref, o_ref, *, tq, scale, hp):
    g = pl.program_id(1)
    a_coef = scale * _LOG2E
    inv_scale = 1.0 / scale
    seq = q_ref.shape[0]
    cc = ccol_ref[...]
    row = lax.broadcasted_iota(jnp.int32, (tq, tq), 0)
    col = lax.broadcasted_iota(jnp.int32, (tq, tq), 1)
    ci_all = [_head_column(cc, g * hp + hh) * inv_scale for hh in range(hp)]
    cj_all = [crow_ref[hh] * inv_scale for hh in range(hp)]
    starts = list(range(0, seq, tq))
    pairs = [starts[i:i + 1] + starts[len(starts) - 1 - i:len(starts) - i]
             for i in range(len(starts) // 2)] or [starts]
    for pair in pairs:
        items = [(hh, r0) for r0 in pair for hh in range(hp)]
        sls = {hh: slice(hh * LANES, (hh + 1) * LANES) for hh in range(hp)}
        u = {}
        for (hh, r0) in items:
            w = r0 + tq
            u[hh, r0] = _dot_nt(q_ref[r0:w, sls[hh]], kb_ref[0:w, sls[hh]]) - cj_all[hh][:, 0:w]
        r = {}
        for (hh, r0) in items:
            w = r0 + tq
            diag = jnp.where(col <= row, u[hh, r0][:, r0:w], NEG_INF)
            u[hh, r0] = diag if r0 == 0 else jnp.concatenate([u[hh, r0][:, 0:r0], diag], axis=1)
            ci = ci_all[hh][r0:w, :]
            m = jnp.max(u[hh, r0], axis=1, keepdims=True) + ci
            r[hh, r0] = m - ci
        p, l = {}, {}
        for t in items:
            pt = jnp.exp2(a_coef * (u[t] - r[t]))
            l[t] = jnp.sum(pt, axis=1, keepdims=True)
            p[t] = pt.astype(BF16)
        for (hh, r0) in items:
            w = r0 + tq
            o = _dot(p[hh, r0], vb_ref[0:w, sls[hh]]) / l[hh, r0]
            o_ref[r0:w, sls[hh]] = o.astype(o_ref.dtype)


def _fox_attn(q, kb, vb, c_row, c_col, batch, seq, heads):
    m, d = q.shape
    hp = 2 if heads % 2 == 0 else 1
    tq = _pick_tile(seq, (256, 128))
    ng = heads // hp
    w = hp * LANES
    c_row = c_row.reshape(batch * ng, hp, 1, seq)
    return pl.pallas_call(
        functools.partial(_fox_attn_kernel, tq=tq, scale=LANES ** -0.5, hp=hp),
        grid=(batch, ng),
        in_specs=[
            pl.BlockSpec((seq, w), lambda b, g: (b, g)),
            pl.BlockSpec((seq, w), lambda b, g: (b, g)),
            pl.BlockSpec((seq, w), lambda b, g: (b, g)),
            pl.BlockSpec((None, hp, 1, seq), lambda b, g: (b * ng + g, 0, 0, 0)),
            pl.BlockSpec((seq, heads), lambda b, g: (b, 0)),
        ],
        out_specs=pl.BlockSpec((seq, w), lambda b, g: (b, g)),
        out_shape=jax.ShapeDtypeStruct((m, d), BF16),
        compiler_params=_cparams("parallel", "parallel"),
        name="fox_attn",
    )(q, kb, vb, c_row, c_col)


def _fox_attn_sample_kernel(q_ref, kn_ref, vn_ref, kp_ref, vp_ref, cpast_ref, cnew_ref, ccol_ref,
                            o_ref, *, scale, heads):
    a_coef = scale * _LOG2E
    inv_scale = 1.0 / scale
    ts = q_ref.shape[0]
    past = kp_ref.shape[0] // heads
    cc = ccol_ref[...]
    row = lax.broadcasted_iota(jnp.int32, (ts, ts), 0)
    col = lax.broadcasted_iota(jnp.int32, (ts, ts), 1)
    group = _pick_tile(heads, (4, 2, 1))
    for h0 in range(0, heads, group):
        hs = list(range(h0, h0 + group))
        sls = {h: slice(h * LANES, (h + 1) * LANES) for h in hs}
        kp = {h: kp_ref[pl.ds(h, past, stride=heads), :].astype(BF16) for h in hs}
        vp = {h: vp_ref[pl.ds(h, past, stride=heads), :].astype(BF16) for h in hs}
        up = {h: _dot_nt(q_ref[:, sls[h]], kp[h]) - cpast_ref[h] * inv_scale for h in hs}
        un = {h: jnp.where(col <= row, _dot_nt(q_ref[:, sls[h]], kn_ref[:, sls[h]]) - cnew_ref[h] * inv_scale,
                           NEG_INF) for h in hs}
        r = {}
        for h in hs:
            ci = _head_column(cc, h) * inv_scale
            m = jnp.maximum(jnp.max(up[h], axis=1, keepdims=True), jnp.max(un[h], axis=1, keepdims=True)) + ci
            r[h] = m - ci
        for h in hs:
            pp = jnp.exp2(a_coef * (up[h] - r[h]))
            pn = jnp.exp2(a_coef * (un[h] - r[h]))
            l = jnp.sum(pp, axis=1, keepdims=True) + jnp.sum(pn, axis=1, keepdims=True)
            o = (_dot(pp.astype(BF16), vp[h]) + _dot(pn.astype(BF16), vn_ref[:, sls[h]])) / l
            o_ref[:, sls[h]] = o.astype(o_ref.dtype)


def _fox_attn_sample(q, kb, vb, k_past, v_past, layer, c_past, c_new, c_col, batch, ts, past, heads):
    m, d = q.shape
    c_past = c_past.reshape(batch, heads, 1, past)
    c_new = c_new.reshape(batch, heads, 1, ts)
    return pl.pallas_call(
        functools.partial(_fox_attn_sample_kernel, scale=LANES ** -0.5, heads=heads),
        grid=(batch,),
        in_specs=[
            pl.BlockSpec((ts, d), lambda b: (b, 0)),
            pl.BlockSpec((ts, d), lambda b: (b, 0)),
            pl.BlockSpec((ts, d), lambda b: (b, 0)),
            pl.BlockSpec((None, past * heads, LANES), lambda b: (layer, b, 0)),
            pl.BlockSpec((None, past * heads, LANES), lambda b: (layer, b, 0)),
            pl.BlockSpec((None, heads, 1, past), lambda b: (b, 0, 0, 0)),
            pl.BlockSpec((None, heads, 1, ts), lambda b: (b, 0, 0, 0)),
            pl.BlockSpec((ts, heads), lambda b: (b, 0)),
        ],
        out_specs=pl.BlockSpec((ts, d), lambda b: (b, 0)),
        out_shape=jax.ShapeDtypeStruct((m, d), BF16),
        compiler_params=_cparams("parallel"),
        name="fox_attn_sample",
    )(q, kb, vb, k_past, v_past, c_past, c_new, c_col)


def _expand_rows(f, reps):
    parts = [jnp.broadcast_to(f[r:r + 1, :], (reps, f.shape[1])) for r in range(f.shape[0])]
    return parts[0] if len(parts) == 1 else jnp.concatenate(parts, axis=0)


def _take_rows(f, idx):
    parts = [f[r:r + 1, :] for r in idx]
    return parts[0] if len(parts) == 1 else jnp.concatenate(parts, axis=0)


def _hgrn_group(q, z, v, go, states, bscr_ref, lb, ng, kscale, n):
    n_g = q.shape[0] // n
    hp = q.shape[1] // LANES
    nblk = n // HGRN_SUB
    tiles = [(u, hh) for u in range(n_g) for hh in range(hp)]

    def tile(x, u, hh):
        return x[u * n:(u + 1) * n, hh * LANES:(hh + 1) * LANES]

    e = jnp.exp(-jnp.abs(z))
    r1 = 1.0 / (1.0 + e)
    er1 = e * r1
    pos = z > 0.0
    g = jnp.log(lb + (1.0 - lb) * jnp.where(pos, r1, er1))
    kk = (1.0 - lb) * jnp.where(pos, er1, r1)
    qf = q * _sigmoid(q) * kscale
    gate = go * _sigmoid(go)
    vb = v.astype(BF16)

    row = lax.broadcasted_iota(jnp.int32, (n, n), 0)
    col = lax.broadcasted_iota(jnp.int32, (n, n), 1)
    tri = jnp.where(col <= row, 1.0, 0.0).astype(BF16)
    g_hi = g.astype(BF16)
    g_r1 = g - g_hi.astype(F32)
    g_mid = g_r1.astype(BF16)
    g_lo = (g_r1 - g_mid.astype(F32)).astype(BF16)
    b = {t: _dot(tri, tile(g_hi, *t)) + _dot(tri, tile(g_mid, *t)) + _dot(tri, tile(g_lo, *t))
         for t in tiles}
    for (u, hh) in tiles:
        bscr_ref[hh, u] = b[u, hh]
    bmid = {(u, hh): bscr_ref[hh, u, pl.ds(HGRN_SUB // 2, nblk, stride=HGRN_SUB), :] for (u, hh) in tiles}
    bend = {(u, hh): bscr_ref[hh, u, pl.ds(HGRN_SUB - 1, nblk, stride=HGRN_SUB), :] for (u, hh) in tiles}
    blast = {t: bend[t][nblk - 1:nblk, :] for t in tiles}
    qd, kd = {}, {}
    for t in tiles:
        bmid_rows = _expand_rows(bmid[t], HGRN_SUB)
        qd[t] = tile(qf, *t) * jnp.exp(b[t] - bmid_rows)
        kd[t] = tile(kk, *t) * jnp.exp(bmid_rows - b[t])

    blk = lax.broadcasted_iota(jnp.int32, (nblk, LANES), 0)
    a = {t: None for t in tiles}
    s = nblk // 2
    while s >= 1:
        upper = ((blk >> (s.bit_length() - 1)) & 1) == 1
        group_shift = (2 * s * HGRN_SUB).bit_length() - 1
        same_group = (row >> group_shift) == (col >> group_shift)
        for t in tiles:
            ref = _take_rows(bend[t], [(i // (2 * s)) * 2 * s + s - 1 for i in range(nblk)])
            fq = jnp.where(upper, jnp.exp(jnp.minimum(bmid[t] - ref, 0.0)), 0.0)
            fk = jnp.where(upper, 0.0, jnp.exp(jnp.minimum(ref - bmid[t], 0.0)))
            ql = qd[t] * _expand_rows(fq, HGRN_SUB)
            kl = kd[t] * _expand_rows(fk, HGRN_SUB)
            prod = _dot_nt(ql.astype(BF16), kl.astype(BF16))
            a[t] = prod if a[t] is None else jnp.where(same_group, prod, a[t])
        s //= 2
    sub_shift = HGRN_SUB.bit_length() - 1
    on_diag = ((row >> sub_shift) == (col >> sub_shift)) & (col <= row)
    o_in, q_in, upd, decay = {}, {}, {}, {}
    for t in tiles:
        pd = _dot_nt(qd[t].astype(BF16), kd[t].astype(BF16))
        at = jnp.where(on_diag, pd, 0.0 if a[t] is None else a[t])
        vt = tile(vb, *t)
        o_in[t] = _dot(at.astype(BF16), vt)
        q_in[t] = (qd[t] * _expand_rows(jnp.exp(bmid[t]), HGRN_SUB)).astype(BF16)
        k_out = kd[t] * _expand_rows(jnp.exp(blast[t] - bmid[t]), HGRN_SUB)
        upd[t] = _dot_tn(vt, k_out.astype(BF16))
        decay[t] = jnp.exp(blast[t])

    states = list(states)
    rows_out = []
    for u in range(n_g):
        cols_out = []
        for hh in range(hp):
            t = (u, hh)
            o = o_in[t] + _dot_nt(q_in[t], states[hh].astype(BF16))
            states[hh] = states[hh] * decay[t] + upd[t]
            o = o * lax.rsqrt(jnp.mean(o * o, axis=-1, keepdims=True) + RMS_EPS) * ng
            cols_out.append(o * tile(gate, *t))
        rows_out.append(cols_out[0] if hp == 1 else jnp.concatenate(cols_out, axis=1))
    out = rows_out[0] if n_g == 1 else jnp.concatenate(rows_out, axis=0)
    return out, states


def _hgrn_kernel(*refs, chunk, n_chunks, has_s0, layer, kscale, hp):
    if has_s0:
        q_ref, z_ref, i_ref, g_ref, lbp_ref, ng_ref, s0_ref, o_ref, s_ref, bscr_ref = refs
    else:
        q_ref, z_ref, i_ref, g_ref, lbp_ref, ng_ref, o_ref, s_ref, bscr_ref = refs
        s0_ref = None

    lbp = lbp_ref[...]
    e = jnp.exp(lbp - jnp.max(lbp, axis=0, keepdims=True))
    soft = e / jnp.sum(e, axis=0, keepdims=True)
    r = lax.broadcasted_iota(jnp.int32, lbp.shape, 0)
    lb_all = jnp.sum(jnp.where((r >= 1) & (r <= layer), soft, 0.0), axis=0, keepdims=True)
    ng = ng_ref[...]

    group = bscr_ref.shape[1]

    def step(t, states):
        rows = pl.ds(pl.multiple_of(t * group * chunk, group * chunk), group * chunk)
        o, states = _hgrn_group(q_ref[rows, :], z_ref[rows, :], i_ref[rows, :], g_ref[rows, :],
                                states, bscr_ref, lb_all, ng, kscale, chunk)
        o_ref[rows, :] = o.astype(o_ref.dtype)
        return tuple(states)

    if has_s0:
        init = tuple(s0_ref[hh].T for hh in range(hp))
    else:
        init = tuple(jnp.zeros((LANES, LANES), F32) for _ in range(hp))
    if n_chunks == group:
        states = step(0, init)
    else:
        states = lax.fori_loop(0, n_chunks // group, step, init)
    for hh in range(hp):
        s_ref[hh] = states[hh].T


def _hgrn(proj, lb_param, norm_g, s0, batch, seq, heads, layer, mixer_idx):
    m = proj.shape[0]
    d = heads * LANES
    chunk = _pick_tile(seq, (HGRN_CHUNK, 64, 32, 16))
    assert chunk % HGRN_SUB == 0 and seq % chunk == 0
    depth = lb_param.shape[0]
    has_s0 = s0 is not None
    hp = _pick_tile(heads, (2, 1) if seq > chunk else (4, 2, 1))
    group = _pick_tile(seq // chunk, (4, 2, 1))
    ng = heads // hp
    w = hp * LANES
    in_specs = [
        pl.BlockSpec((seq, w), lambda b, g: (b, g)),
        pl.BlockSpec((seq, w), lambda b, g: (b, ng + g)),
        pl.BlockSpec((seq, w), lambda b, g: (b, 2 * ng + g)),
        pl.BlockSpec((seq, w), lambda b, g: (b, 3 * ng + g)),
        pl.BlockSpec((depth, w), lambda b, g: (0, g)),
        pl.BlockSpec((None, 1, LANES), lambda b, g: (mixer_idx, 0, 0)),
    ]
    args = [proj, proj, proj, proj, lb_param, norm_g]
    if has_s0:
        in_specs.append(pl.BlockSpec((None, hp, LANES, LANES), lambda b, g: (b, g, 0, 0)))
        args.append(s0)
    return pl.pallas_call(
        functools.partial(_hgrn_kernel, chunk=chunk, n_chunks=seq // chunk, has_s0=has_s0,
                          layer=layer, kscale=LANES ** -0.5, hp=hp),
        grid=(batch, ng),
        in_specs=in_specs,
        out_specs=[
            pl.BlockSpec((seq, w), lambda b, g: (b, g)),
            pl.BlockSpec((None, hp, LANES, LANES), lambda b, g: (b, g, 0, 0)),
        ],
        out_shape=[
            jax.ShapeDtypeStruct((m, d), BF16),
            jax.ShapeDtypeStruct((batch, heads, LANES, LANES), F32),
        ],
        scratch_shapes=[pltpu.VMEM((hp, group, chunk, LANES), F32)],
        compiler_params=_cparams("parallel", "parallel"),
        name="hgrn",
    )(*args)


def _outproj_ln_kernel(o_ref, w_ref, x_ref, g_ref, b_ref, y_ref, *, alpha, rows):
    for r0 in range(0, o_ref.shape[0], rows):
        sl = slice(r0, r0 + rows)
        r = alpha * x_ref[sl, :] + _dot(o_ref[sl, :], w_ref[...])
        y_ref[sl, :] = _layer_norm(r, g_ref[...], b_ref[...])


def _outproj_ln(o, w, x, ln_g, ln_b, layer, ln_idx, alpha):
    m, d = x.shape
    tm = _pick_tile(m, (512, 256, 128, 64, 32, 16))
    return pl.pallas_call(
        functools.partial(_outproj_ln_kernel, alpha=alpha, rows=_pick_tile(tm, (128,))),
        grid=(m // tm,),
        in_specs=[
            pl.BlockSpec((tm, d), lambda i: (i, 0)),
            pl.BlockSpec((None, d, d), lambda i: (layer, 0, 0)),
            pl.BlockSpec((tm, d), lambda i: (i, 0)),
            pl.BlockSpec((None, 1, d), lambda i: (ln_idx, 0, 0)),
            pl.BlockSpec((None, 1, d), lambda i: (ln_idx, 0, 0)),
        ],
        out_specs=pl.BlockSpec((tm, d), lambda i: (i, 0)),
        out_shape=jax.ShapeDtypeStruct((m, d), F32),
        compiler_params=_cparams("parallel"),
        name="outproj_ln",
    )(o, w, x, ln_g, ln_b)


def _prep_ffn(w_up, w_down, tf):
    f = w_down.shape[1]
    fp = _round_up(f, tf)
    pad = ((0, 0), (0, 0), (0, fp - f))
    gate = jnp.pad(w_up[..., :f].astype(BF16), pad)
    up = jnp.pad(w_up[..., f:].astype(BF16), pad)
    down = jnp.pad(w_down.astype(BF16), ((0, 0), (0, fp - f), (0, 0)))
    return gate, up, down


def _cumsum_heads(logf_bsh):
    b, s, h = logf_bsh.shape
    sp = _round_up(s, LANES)
    x = jnp.transpose(logf_bsh, (0, 2, 1)).reshape(b * h, s)
    x = jnp.pad(x, ((0, 0), (0, sp - s)))
    return _cumsum_lanes(x)[:, :s]


def kernel(x_prompt, x_sample, cache_fox_k, cache_fox_v, cache_fox_logf, state_hgrn, ln_g, ln_b, ffn1_up, ffn1_down, ffn2_up, ffn2_down, fox_w_in, fox_b_f, fox_w_out, hgrn_w_in, hgrn_lb, hgrn_norm_g, hgrn_w_out):
    bp, tp, d = x_prompt.shape
    bs, ts, _ = x_sample.shape
    depth = ln_g.shape[0]
    heads = fox_b_f.shape[1]
    n_fox = fox_w_in.shape[0]
    past = cache_fox_k.shape[2]
    assert d == heads * LANES and hgrn_norm_g.shape[1] == LANES
    alpha = (2 * depth) ** 0.25

    tf = 512 if ffn1_down.shape[1] >= 512 else LANES
    ffn1 = _prep_ffn(ffn1_up, ffn1_down, tf)
    ffn2 = _prep_ffn(ffn2_up, ffn2_down, tf)
    fox_qkv = fox_w_in[..., :3 * d].astype(BF16)
    fox_wf = jnp.pad(fox_w_in[..., 3 * d:], ((0, 0), (0, 0), (0, LANES - heads))).astype(BF16)
    fox_bf = jnp.pad(fox_b_f, ((0, 0), (0, LANES - heads)))[:, None, :]
    fox_out = fox_w_out.astype(BF16)
    hgrn_in = hgrn_w_in.astype(BF16)
    hgrn_out = hgrn_w_out.astype(BF16)
    hgrn_ng = hgrn_norm_g[:, None, :]
    g3 = ln_g.reshape(depth * 3, 1, d)
    b3 = ln_b.reshape(depth * 3, 1, d)

    xp = x_prompt.reshape(bp * tp, d)
    xs = x_sample.reshape(bs * ts, d)
    kv_p, kv_s = None, None
    logf_p, logf_s, hg_p, hg_s = [], [], [], []

    for i in range(depth):
        j = i // 2
        xp, xpb = _ffn_ln(xp, ffn1, g3, b3, i, 3 * i, alpha, tf, with_bf16=True)
        xs, xsb = _ffn_ln(xs, ffn1, g3, b3, i, 3 * i, alpha, tf, with_bf16=True)
        if i % 2 == 0:
            q, kb, vb, k_st, v_st, lf = _fox_proj(xpb, fox_qkv, fox_wf, fox_bf, j, n_fox, heads, kv_p)
            kv_p = (k_st, v_st)
            logf = lf[:, :heads].reshape(bp, tp, heads)
            c_row = _cumsum_heads(logf)
            c_col = jnp.transpose(c_row.reshape(bp, heads, tp), (0, 2, 1)).reshape(bp * tp, heads)
            op = _fox_attn(q, kb, vb, c_row, c_col, bp, tp, heads)
            logf_p.append(logf)
            q, kb, vb, k_st, v_st, lf = _fox_proj(xsb, fox_qkv, fox_wf, fox_bf, j, n_fox, heads, kv_s)
            kv_s = (k_st, v_st)
            logf = lf[:, :heads].reshape(bs, ts, heads)
            c_all = _cumsum_heads(jnp.concatenate([cache_fox_logf[j].astype(F32), logf], axis=1))
            c_new = c_all[:, past:]
            c_col = jnp.transpose(c_new.reshape(bs, heads, ts), (0, 2, 1)).reshape(bs * ts, heads)
            os_ = _fox_attn_sample(q, kb, vb, cache_fox_k.reshape(n_fox, bs * past * heads, LANES),
                                   cache_fox_v.reshape(n_fox, bs * past * heads, LANES), j,
                                   c_all[:, :past], c_new, c_col, bs, ts, past, heads)
            logf_s.append(logf)
            w_out = fox_out
        else:
            pp = _proj(xpb, hgrn_in, j, 4 * d)
            op, sp = _hgrn(pp, hgrn_lb, hgrn_ng, None, bp, tp, heads, i, j)
            ps = _proj(xsb, hgrn_in, j, 4 * d)
            os_, ss = _hgrn(ps, hgrn_lb, hgrn_ng, state_hgrn[j], bs, ts, heads, i, j)
            hg_p.append(sp)
            hg_s.append(ss)
            w_out = hgrn_out
        xp = _outproj_ln(op, w_out, xp, g3, b3, j, 3 * i + 1, alpha)
        xs = _outproj_ln(os_, w_out, xs, g3, b3, j, 3 * i + 1, alpha)
        xp = _ffn_ln(xp, ffn2, g3, b3, i, 3 * i + 2, alpha, tf)
        xs = _ffn_ln(xs, ffn2, g3, b3, i, 3 * i + 2, alpha, tf)

    return (xp.reshape(bp, tp, d), xs.reshape(bs, ts, d),
            kv_p[0].reshape(n_fox, bp, tp, heads, LANES), kv_p[1].reshape(n_fox, bp, tp, heads, LANES),
            jnp.stack(logf_p), jnp.stack(hg_p),
            kv_s[0].reshape(n_fox, bs, ts, heads, LANES), kv_s[1].reshape(n_fox, bs, ts, heads, LANES),
            jnp.stack(logf_s), jnp.stack(hg_s))
```

```python
import functools
import math

import jax
import jax.numpy as jnp
from jax import lax
from jax.experimental import pallas as pl
from jax.experimental.pallas import tpu as pltpu

F32 = jnp.float32
BF16 = jnp.bfloat16

LN_EPS = 1e-5
RMS_EPS = 1e-6
NEG_INF = -1e30

LANES = 128
HGRN_SUB = 16
HGRN_CHUNK = 128
VMEM_LIMIT_BYTES = 48 * 1024 * 1024


def _round_up(n, m):
    return (n + m - 1) // m * m


def _pick_tile(n, candidates):
    for c in candidates:
        if n % c == 0:
            return c
    return n


def _cparams(*semantics):
    return pltpu.CompilerParams(dimension_semantics=semantics,
                                vmem_limit_bytes=VMEM_LIMIT_BYTES)


def _dot(a, b):
    return jnp.dot(a, b, preferred_element_type=F32)


def _dot_nt(a, b):
    return lax.dot_general(a, b, (((1,), (1,)), ((), ())), preferred_element_type=F32)


def _dot_tn(a, b):
    return lax.dot_general(a, b, (((0,), (0,)), ((), ())), preferred_element_type=F32)


def _sigmoid(x):
    return 1.0 / (1.0 + jnp.exp(-x))


def _log_sigmoid(x):
    return jnp.minimum(x, 0.0) - jnp.log(1.0 + jnp.exp(-jnp.abs(x)))


def _layer_norm(r, g, b):
    mu = jnp.mean(r, axis=-1, keepdims=True)
    c = r - mu
    var = jnp.mean(c * c, axis=-1, keepdims=True)
    return c * lax.rsqrt(var + LN_EPS) * g + b


def _ffn_ln_kernel(x_ref, wg_ref, wu_ref, wd_ref, g_ref, b_ref, *rest, alpha, n_tiles, ln_rows):
    out_refs, (xb_ref, acc_ref) = rest[:-2], rest[-2:]
    i = pl.program_id(0)
    f = pl.program_id(1)
    slot = lax.rem(i, 2)
    tm = xb_ref.shape[0]

    def norm_rows(src_slot):
        start = pl.multiple_of(jnp.minimum(f * ln_rows, tm - ln_rows), 16)
        rows = pl.ds(start, ln_rows)
        y = _layer_norm(0.5 * acc_ref[src_slot, rows, :], g_ref[...], b_ref[...])
        for o_ref in out_refs:
            o_ref[rows, :] = y.astype(o_ref.dtype)

    @pl.when((i == 0) & (f == 0))
    def _():
        acc_ref[1] = jnp.zeros(acc_ref.shape[1:], F32)

    @pl.when(i < n_tiles)
    def _():
        @pl.when(f == 0)
        def _():
            x = x_ref[...]
            xb_ref[...] = x.astype(BF16)
            acc_ref[slot] = (2.0 * alpha) * x

        norm_rows(1 - slot)
        xb = xb_ref[...]
        gate = _dot(xb, wg_ref[...])
        up = _dot(xb, wu_ref[...])
        h = gate * _sigmoid(gate) * up
        acc_ref[slot] += _dot(h.astype(BF16), wd_ref[...])

    @pl.when(i == n_tiles)
    def _():
        norm_rows(1 - slot)


def _ffn_ln(x, weights, ln_g, ln_b, layer, ln_idx, alpha, tf, with_bf16=False):
    w_gate, w_up, w_down = weights
    m, d = x.shape
    fp = w_down.shape[1]
    nf = fp // tf
    tm = _pick_tile(m, (512, 256, 128, 64, 32, 16))
    nt = m // tm
    ln_rows = min(tm, _round_up(-(-tm // nf), 16))
    out_dtypes = (F32, BF16) if with_bf16 else (F32,)

    def w_step(i, f):
        return jnp.where(i < nt, f, nf - 1)

    outs = pl.pallas_call(
        functools.partial(_ffn_ln_kernel, alpha=alpha, n_tiles=nt, ln_rows=ln_rows),
        grid=(nt + 1, nf),
        in_specs=[
            pl.BlockSpec((tm, d), lambda i, f: (jnp.minimum(i, nt - 1), 0)),
            pl.BlockSpec((None, d, tf), lambda i, f: (layer, 0, w_step(i, f))),
            pl.BlockSpec((None, d, tf), lambda i, f: (layer, 0, w_step(i, f))),
            pl.BlockSpec((None, tf, d), lambda i, f: (layer, w_step(i, f), 0)),
            pl.BlockSpec((None, 1, d), lambda i, f: (ln_idx, 0, 0)),
            pl.BlockSpec((None, 1, d), lambda i, f: (ln_idx, 0, 0)),
        ],
        out_specs=[pl.BlockSpec((tm, d), lambda i, f: (jnp.maximum(i - 1, 0), 0)) for _ in out_dtypes],
        out_shape=[jax.ShapeDtypeStruct((m, d), dt) for dt in out_dtypes],
        scratch_shapes=[pltpu.VMEM((tm, d), BF16), pltpu.VMEM((2, tm, d), F32)],
        compiler_params=_cparams("arbitrary", "arbitrary"),
        name="ffn_ln",
    )(x, w_gate, w_up, w_down, ln_g, ln_b)
    return outs if with_bf16 else outs[0]


def _proj_kernel(x_ref, w_ref, o_ref):
    o_ref[...] = _dot(x_ref[...], w_ref[...]).astype(o_ref.dtype)


def _proj(xb, w, layer, n, out_dtype=F32):
    m, d = xb.shape
    tm = _pick_tile(m, (2048, 1024, 512, 256, 128, 64, 32, 16))
    tn = _pick_tile(n, (512, 256, 128))
    return pl.pallas_call(
        _proj_kernel,
        grid=(m // tm, n // tn),
        in_specs=[
            pl.BlockSpec((tm, d), lambda i, j: (i, 0)),
            pl.BlockSpec((None, d, tn), lambda i, j: (layer, 0, j)),
        ],
        out_specs=pl.BlockSpec((tm, tn), lambda i, j: (i, j)),
        out_shape=jax.ShapeDtypeStruct((m, n), out_dtype),
        compiler_params=_cparams("parallel", "arbitrary"),
        name="proj",
    )(xb, w)


def _fox_proj_kernel(x_ref, w_ref, wf_ref, bf_ref, *rest, nq, heads):
    q_ref, kb_ref, vb_ref, k_ref, v_ref, lf_ref = rest[-6:]
    c = pl.program_id(0)
    tm, tn = q_ref.shape

    def cache_rows(o_ref, y):
        o_ref[...] = y.reshape(tm, tn // LANES, LANES)

    @pl.when(c < nq)
    def _():
        q_ref[...] = _dot(x_ref[...], w_ref[...]).astype(q_ref.dtype)

    @pl.when((c >= nq) & (c < 2 * nq))
    def _():
        y = _dot(x_ref[...], w_ref[...])
        kb_ref[...] = y.astype(kb_ref.dtype)
        cache_rows(k_ref, y)

    @pl.when((c >= 2 * nq) & (c < 3 * nq))
    def _():
        y = _dot(x_ref[...], w_ref[...])
        vb_ref[...] = y.astype(vb_ref.dtype)
        cache_rows(v_ref, y)

    @pl.when(c == 3 * nq)
    def _():
        lf_ref[...] = _log_sigmoid(_dot(x_ref[...], wf_ref[...]) + bf_ref[...])


def _fox_proj(xb, w_qkv, wf, bf, layer, n_layers, heads, kv_stacks):
    m, d = xb.shape
    tm = _pick_tile(m, (512, 256, 128, 64, 32, 16))
    tn = 1024 if d % 1024 == 0 else d
    nq = d // tn
    hpt = tn // LANES
    assert hpt % 8 == 0 or hpt == heads
    nt = m // tm

    def rows(c, i, lo):
        return jnp.where(c < lo, 0, jnp.where(c < lo + nq, i, nt - 1))

    def cols(c, lo):
        return jnp.clip(c - lo, 0, nq - 1)

    in_specs = [
        pl.BlockSpec((tm, d), lambda c, i: (i, 0)),
        pl.BlockSpec((None, d, tn), lambda c, i: (layer, 0, jnp.minimum(c, 3 * nq - 1))),
        pl.BlockSpec((None, d, LANES), lambda c, i: (layer, 0, 0)),
        pl.BlockSpec((None, 1, LANES), lambda c, i: (layer, 0, 0)),
    ]
    args = [xb, w_qkv, wf, bf]
    aliases = {}
    if kv_stacks is not None:
        in_specs += [pl.BlockSpec(memory_space=pl.ANY), pl.BlockSpec(memory_space=pl.ANY)]
        args += list(kv_stacks)
        aliases = {4: 3, 5: 4}
    return pl.pallas_call(
        functools.partial(_fox_proj_kernel, nq=nq, heads=heads),
        grid=(3 * nq + 1, nt),
        in_specs=in_specs,
        out_specs=[
            pl.BlockSpec((tm, tn), lambda c, i: (rows(c, i, 0), cols(c, 0))),
            pl.BlockSpec((tm, tn), lambda c, i: (rows(c, i, nq), cols(c, nq))),
            pl.BlockSpec((tm, tn), lambda c, i: (rows(c, i, 2 * nq), cols(c, 2 * nq))),
            pl.BlockSpec((None, tm, hpt, LANES), lambda c, i: (layer, rows(c, i, nq), cols(c, nq), 0)),
            pl.BlockSpec((None, tm, hpt, LANES), lambda c, i: (layer, rows(c, i, 2 * nq), cols(c, 2 * nq), 0)),
            pl.BlockSpec((tm, LANES), lambda c, i: (jnp.where(c == 3 * nq, i, 0), 0)),
        ],
        out_shape=[
            jax.ShapeDtypeStruct((m, d), BF16),
            jax.ShapeDtypeStruct((m, d), BF16),
            jax.ShapeDtypeStruct((m, d), BF16),
            jax.ShapeDtypeStruct((n_layers, m, heads, LANES), F32),
            jax.ShapeDtypeStruct((n_layers, m, heads, LANES), F32),
            jax.ShapeDtypeStruct((m, LANES), F32),
        ],
        input_output_aliases=aliases,
        compiler_params=_cparams("arbitrary", "arbitrary"),
        name="fox_proj",
    )(*args)


def _cumsum_lanes_kernel(x_ref, o_ref):
    x = x_ref[...]
    n = x.shape[-1]
    lane = lax.broadcasted_iota(jnp.int32, x.shape, 1)
    s = 1
    while s < n:
        x = x + jnp.where(lane >= s, pltpu.roll(x, s, 1), 0.0)
        s *= 2
    o_ref[...] = x


def _cumsum_lanes(x):
    r, n = x.shape
    rb = _pick_tile(r, (64, 32, 16, 8))
    return pl.pallas_call(
        _cumsum_lanes_kernel,
        grid=(r // rb,),
        in_specs=[pl.BlockSpec((rb, n), lambda i: (i, 0))],
        out_specs=pl.BlockSpec((rb, n), lambda i: (i, 0)),
        out_shape=jax.ShapeDtypeStruct((r, n), F32),
        compiler_params=_cparams("parallel"),
        name="cumsum_lanes",
    )(x)


def _head_column(c, h):
    lane = lax.broadcasted_iota(jnp.int32, c.shape, 1)
    return jnp.sum(jnp.where(lane == h, c, 0.0), axis=1, keepdims=True)


_LOG2E = math.log2(math.e)


def _fox_attn_kernel(q_ref, kb_ref, vb_ref, crow_ref, ccol_ref, o_ref, *, tq, scale, hp):
    g = pl.program_id(1)
    a_coef = scale * _LOG2E
    inv_scale = 1.0 / scale
    seq = q_ref.shape[0]
    cc = ccol_ref[...]
    row = lax.broadcasted_iota(jnp.int32, (tq, tq), 0)
    col = lax.broadcasted_iota(jnp.int32, (tq, tq), 1)
    ci_all = [_head_column(cc, g * hp + hh) * inv_scale for hh in range(hp)]
    cj_all = [crow_ref[hh] * inv_scale for hh in range(hp)]
    starts = list(range(0, seq, tq))
    pairs = [starts[i:i + 1] + starts[len(starts) - 1 - i:len(starts) - i]
             for i in range(len(starts) // 2)] or [starts]
    for pair in pairs:
        items = [(hh, r0) for r0 in pair for hh in range(hp)]
        sls = {hh: slice(hh * LANES, (hh + 1) * LANES) for hh in range(hp)}
        u = {}
        for (hh, r0) in items:
            w = r0 + tq
            u[hh, r0] = _dot_nt(q_ref[r0:w, sls[hh]], kb_ref[0:w, sls[hh]]) - cj_all[hh][:, 0:w]
        r = {}
        for (hh, r0) in items:
            w = r0 + tq
            diag = jnp.where(col <= row, u[hh, r0][:, r0:w], NEG_INF)
            u[hh, r0] = diag if r0 == 0 else jnp.concatenate([u[hh, r0][:, 0:r0], diag], axis=1)
            ci = ci_all[hh][r0:w, :]
            m = jnp.max(u[hh, r0], axis=1, keepdims=True) + ci
            r[hh, r0] = m - ci
        p, l = {}, {}
        for t in items:
            pt = jnp.exp2(a_coef * (u[t] - r[t]))
            l[t] = jnp.sum(pt, axis=1, keepdims=True)
            p[t] = pt.astype(BF16)
        for (hh, r0) in items:
            w = r0 + tq
            o = _dot(p[hh, r0], vb_ref[0:w, sls[hh]]) / l[hh, r0]
            o_ref[r0:w, sls[hh]] = o.astype(o_ref.dtype)


def _fox_attn(q, kb, vb, c_row, c_col, batch, seq, heads):
    m, d = q.shape
    hp = 2 if heads % 2 == 0 else 1
    tq = _pick_tile(seq, (256, 128))
    ng = heads // hp
    w = hp * LANES
    c_row = c_row.reshape(batch * ng, hp, 1, seq)
    return pl.pallas_call(
        functools.partial(_fox_attn_kernel, tq=tq, scale=LANES ** -0.5, hp=hp),
        grid=(batch, ng),
        in_specs=[
            pl.BlockSpec((seq, w), lambda b, g: (b, g)),
            pl.BlockSpec((seq, w), lambda b, g: (b, g)),
            pl.BlockSpec((seq, w), lambda b, g: (b, g)),
            pl.BlockSpec((None, hp, 1, seq), lambda b, g: (b * ng + g, 0, 0, 0)),
            pl.BlockSpec((seq, heads), lambda b, g: (b, 0)),
        ],
        out_specs=pl.BlockSpec((seq, w), lambda b, g: (b, g)),
        out_shape=jax.ShapeDtypeStruct((m, d), BF16),
        compiler_params=_cparams("parallel", "parallel"),
        name="fox_attn",
    )(q, kb, vb, c_row, c_col)


def _fox_attn_sample_kernel(q_ref, kn_ref, vn_ref, kp_ref, vp_ref, cpast_ref, cnew_ref, ccol_ref,
                            o_ref, *, scale, heads):
    a_coef = scale * _LOG2E
    inv_scale = 1.0 / scale
    ts = q_ref.shape[0]
    past = kp_ref.shape[0] // heads
    cc = ccol_ref[...]
    row = lax.broadcasted_iota(jnp.int32, (ts, ts), 0)
    col = lax.broadcasted_iota(jnp.int32, (ts, ts), 1)
    group = _pick_tile(heads, (4, 2, 1))
    for h0 in range(0, heads, group):
        hs = list(range(h0, h0 + group))
        sls = {h: slice(h * LANES, (h + 1) * LANES) for h in hs}
        kp = {h: kp_ref[pl.ds(h, past, stride=heads), :].astype(BF16) for h in hs}
        vp = {h: vp_ref[pl.ds(h, past, stride=heads), :].astype(BF16) for h in hs}
        up = {h: _dot_nt(q_ref[:, sls[h]], kp[h]) - cpast_ref[h] * inv_scale for h in hs}
        un = {h: jnp.where(col <= row, _dot_nt(q_ref[:, sls[h]], kn_ref[:, sls[h]]) - cnew_ref[h] * inv_scale,
                           NEG_INF) for h in hs}
        r = {}
        for h in hs:
            ci = _head_column(cc, h) * inv_scale
            m = jnp.maximum(jnp.max(up[h], axis=1, keepdims=True), jnp.max(un[h], axis=1, keepdims=True)) + ci
            r[h] = m - ci
        for h in hs:
            pp = jnp.exp2(a_coef * (up[h] - r[h]))
            pn = jnp.exp2(a_coef * (un[h] - r[h]))
            l = jnp.sum(pp, axis=1, keepdims=True) + jnp.sum(pn, axis=1, keepdims=True)
            o = (_dot(pp.astype(BF16), vp[h]) + _dot(pn.astype(BF16), vn_ref[:, sls[h]])) / l
            o_ref[:, sls[h]] = o.astype(o_ref.dtype)


def _fox_attn_sample(q, kb, vb, k_past, v_past, layer, c_past, c_new, c_col, batch, ts, past, heads):
    m, d = q.shape
    c_past = c_past.reshape(batch, heads, 1, past)
    c_new = c_new.reshape(batch, heads, 1, ts)
    return pl.pallas_call(
        functools.partial(_fox_attn_sample_kernel, scale=LANES ** -0.5, heads=heads),
        grid=(batch,),
        in_specs=[
            pl.BlockSpec((ts, d), lambda b: (b, 0)),
            pl.BlockSpec((ts, d), lambda b: (b, 0)),
            pl.BlockSpec((ts, d), lambda b: (b, 0)),
            pl.BlockSpec((None, past * heads, LANES), lambda b: (layer, b, 0)),
            pl.BlockSpec((None, past * heads, LANES), lambda b: (layer, b, 0)),
            pl.BlockSpec((None, heads, 1, past), lambda b: (b, 0, 0, 0)),
            pl.BlockSpec((None, heads, 1, ts), lambda b: (b, 0, 0, 0)),
            pl.BlockSpec((ts, heads), lambda b: (b, 0)),
        ],
        out_specs=pl.BlockSpec((ts, d), lambda b: (b, 0)),
        out_shape=jax.ShapeDtypeStruct((m, d), BF16),
        compiler_params=_cparams("parallel"),
        name="fox_attn_sample",
    )(q, kb, vb, k_past, v_past, c_past, c_new, c_col)


def _expand_rows(f, reps):
    parts = [jnp.broadcast_to(f[r:r + 1, :], (reps, f.shape[1])) for r in range(f.shape[0])]
    return parts[0] if len(parts) == 1 else jnp.concatenate(parts, axis=0)


def _take_rows(f, idx):
    parts = [f[r:r + 1, :] for r in idx]
    return parts[0] if len(parts) == 1 else jnp.concatenate(parts, axis=0)


def _hgrn_group(q, z, v, go, states, bscr_ref, lb, ng, kscale, n):
    n_g = q.shape[0] // n
    hp = q.shape[1] // LANES
    nblk = n // HGRN_SUB
    tiles = [(u, hh) for u in range(n_g) for hh in range(hp)]

    def tile(x, u, hh):
        return x[u * n:(u + 1) * n, hh * LANES:(hh + 1) * LANES]

    e = jnp.exp(-jnp.abs(z))
    r1 = 1.0 / (1.0 + e)
    er1 = e * r1
    pos = z > 0.0
    g = jnp.log(lb + (1.0 - lb) * jnp.where(pos, r1, er1))
    kk = (1.0 - lb) * jnp.where(pos, er1, r1)
    qf = q * _sigmoid(q) * kscale
    gate = go * _sigmoid(go)
    vb = v.astype(BF16)

    row = lax.broadcasted_iota(jnp.int32, (n, n), 0)
    col = lax.broadcasted_iota(jnp.int32, (n, n), 1)
    tri = jnp.where(col <= row, 1.0, 0.0).astype(BF16)
    g_hi = g.astype(BF16)
    g_r1 = g - g_hi.astype(F32)
    g_mid = g_r1.astype(BF16)
    g_lo = (g_r1 - g_mid.astype(F32)).astype(BF16)
    b = {t: _dot(tri, tile(g_hi, *t)) + _dot(tri, tile(g_mid, *t)) + _dot(tri, tile(g_lo, *t))
         for t in tiles}
    for (u, hh) in tiles:
        bscr_ref[hh, u] = b[u, hh]
    bmid = {(u, hh): bscr_ref[hh, u, pl.ds(HGRN_SUB // 2, nblk, stride=HGRN_SUB), :] for (u, hh) in tiles}
    bend = {(u, hh): bscr_ref[hh, u, pl.ds(HGRN_SUB - 1, nblk, stride=HGRN_SUB), :] for (u, hh) in tiles}
    blast = {t: bend[t][nblk - 1:nblk, :] for t in tiles}
    qd, kd = {}, {}
    for t in tiles:
        bmid_rows = _expand_rows(bmid[t], HGRN_SUB)
        qd[t] = tile(qf, *t) * jnp.exp(b[t] - bmid_rows)
        kd[t] = tile(kk, *t) * jnp.exp(bmid_rows - b[t])

    blk = lax.broadcasted_iota(jnp.int32, (nblk, LANES), 0)
    a = {t: None for t in tiles}
    s = nblk // 2
    while s >= 1:
        upper = ((blk >> (s.bit_length() - 1)) & 1) == 1
        group_shift = (2 * s * HGRN_SUB).bit_length() - 1
        same_group = (row >> group_shift) == (col >> group_shift)
        for t in tiles:
            ref = _take_rows(bend[t], [(i // (2 * s)) * 2 * s + s - 1 for i in range(nblk)])
            fq = jnp.where(upper, jnp.exp(jnp.minimum(bmid[t] - ref, 0.0)), 0.0)
            fk = jnp.where(upper, 0.0, jnp.exp(jnp.minimum(ref - bmid[t], 0.0)))
            ql = qd[t] * _expand_rows(fq, HGRN_SUB)
            kl = kd[t] * _expand_rows(fk, HGRN_SUB)
            prod = _dot_nt(ql.astype(BF16), kl.astype(BF16))
            a[t] = prod if a[t] is None else jnp.where(same_group, prod, a[t])
        s //= 2
    sub_shift = HGRN_SUB.bit_length() - 1
    on_diag = ((row >> sub_shift) == (col >> sub_shift)) & (col <= row)
    o_in, q_in, upd, decay = {}, {}, {}, {}
    for t in tiles:
        pd = _dot_nt(qd[t].astype(BF16), kd[t].astype(BF16))
        at = jnp.where(on_diag, pd, 0.0 if a[t] is None else a[t])
        vt = tile(vb, *t)
        o_in[t] = _dot(at.astype(BF16), vt)
        q_in[t] = (qd[t] * _expand_rows(jnp.exp(bmid[t]), HGRN_SUB)).astype(BF16)
        k_out = kd[t] * _expand_rows(jnp.exp(blast[t] - bmid[t]), HGRN_SUB)
        upd[t] = _dot_tn(vt, k_out.astype(BF16))
        decay[t] = jnp.exp(blast[t])

    states = list(states)
    rows_out = []
    for u in range(n_g):
        cols_out = []
        for hh in range(hp):
            t = (u, hh)
            o = o_in[t] + _dot_nt(q_in[t], states[hh].astype(BF16))
            states[hh] = states[hh] * decay[t] + upd[t]
            o = o * lax.rsqrt(jnp.mean(o * o, axis=-1, keepdims=True) + RMS_EPS) * ng
            cols_out.append(o * tile(gate, *t))
        rows_out.append(cols_out[0] if hp == 1 else jnp.concatenate(cols_out, axis=1))
    out = rows_out[0] if n_g == 1 else jnp.concatenate(rows_out, axis=0)
    return out, states


def _hgrn_kernel(*refs, chunk, n_chunks, has_s0, layer, kscale, hp):
    if has_s0:
        q_ref, z_ref, i_ref, g_ref, lbp_ref, ng_ref, s0_ref, o_ref, s_ref, bscr_ref = refs
    else:
        q_ref, z_ref, i_ref, g_ref, lbp_ref, ng_ref, o_ref, s_ref, bscr_ref = refs
        s0_ref = None

    lbp = lbp_ref[...]
    e = jnp.exp(lbp - jnp.max(lbp, axis=0, keepdims=True))
    soft = e / jnp.sum(e, axis=0, keepdims=True)
    r = lax.broadcasted_iota(jnp.int32, lbp.shape, 0)
    lb_all = jnp.sum(jnp.where((r >= 1) & (r <= layer), soft, 0.0), axis=0, keepdims=True)
    ng = ng_ref[...]

    group = bscr_ref.shape[1]

    def step(t, states):
        rows = pl.ds(pl.multiple_of(t * group * chunk, group * chunk), group * chunk)
        o, states = _hgrn_group(q_ref[rows, :], z_ref[rows, :], i_ref[rows, :], g_ref[rows, :],
                                states, bscr_ref, lb_all, ng, kscale, chunk)
        o_ref[rows, :] = o.astype(o_ref.dtype)
        return tuple(states)

    if has_s0:
        init = tuple(s0_ref[hh].T for hh in range(hp))
    else:
        init = tuple(jnp.zeros((LANES, LANES), F32) for _ in range(hp))
    if n_chunks == group:
        states = step(0, init)
    else:
        states = lax.fori_loop(0, n_chunks // group, step, init)
    for hh in range(hp):
        s_ref[hh] = states[hh].T


def _hgrn(proj, lb_param, norm_g, s0, batch, seq, heads, layer, mixer_idx):
    m = proj.shape[0]
    d = heads * LANES
    chunk = _pick_tile(seq, (HGRN_CHUNK, 64, 32, 16))
    assert chunk % HGRN_SUB == 0 and seq % chunk == 0
    depth = lb_param.shape[0]
    has_s0 = s0 is not None
    hp = _pick_tile(heads, (2, 1) if seq > chunk else (4, 2, 1))
    group = _pick_tile(seq // chunk, (4, 2, 1))
    ng = heads // hp
    w = hp * LANES
    in_specs = [
        pl.BlockSpec((seq, w), lambda b, g: (b, g)),
        pl.BlockSpec((seq, w), lambda b, g: (b, ng + g)),
        pl.BlockSpec((seq, w), lambda b, g: (b, 2 * ng + g)),
        pl.BlockSpec((seq, w), lambda b, g: (b, 3 * ng + g)),
        pl.BlockSpec((depth, w), lambda b, g: (0, g)),
        pl.BlockSpec((None, 1, LANES), lambda b, g: (mixer_idx, 0, 0)),
    ]
    args = [proj, proj, proj, proj, lb_param, norm_g]
    if has_s0:
        in_specs.append(pl.BlockSpec((None, hp, LANES, LANES), lambda b, g: (b, g, 0, 0)))
        args.append(s0)
    return pl.pallas_call(
        functools.partial(_hgrn_kernel, chunk=chunk, n_chunks=seq // chunk, has_s0=has_s0,
                          layer=layer, kscale=LANES ** -0.5, hp=hp),
        grid=(batch, ng),
        in_specs=in_specs,
        out_specs=[
            pl.BlockSpec((seq, w), lambda b, g: (b, g)),
            pl.BlockSpec((None, hp, LANES, LANES), lambda b, g: (b, g, 0, 0)),
        ],
        out_shape=[
            jax.ShapeDtypeStruct((m, d), BF16),
            jax.ShapeDtypeStruct((batch, heads, LANES, LANES), F32),
        ],
        scratch_shapes=[pltpu.VMEM((hp, group, chunk, LANES), F32)],
        compiler_params=_cparams("parallel", "parallel"),
        name="hgrn",
    )(*args)


def _outproj_ln_kernel(o_ref, w_ref, x_ref, g_ref, b_ref, y_ref, *, alpha, rows):
    for r0 in range(0, o_ref.shape[0], rows):
        sl = slice(r0, r0 + rows)
        r = alpha * x_ref[sl, :] + _dot(o_ref[sl, :], w_ref[...])
        y_ref[sl, :] = _layer_norm(r, g_ref[...], b_ref[...])


def _outproj_ln(o, w, x, ln_g, ln_b, layer, ln_idx, alpha):
    m, d = x.shape
    tm = _pick_tile(m, (512, 256, 128, 64, 32, 16))
    return pl.pallas_call(
        functools.partial(_outproj_ln_kernel, alpha=alpha, rows=_pick_tile(tm, (128,))),
        grid=(m // tm,),
        in_specs=[
            pl.BlockSpec((tm, d), lambda i: (i, 0)),
            pl.BlockSpec((None, d, d), lambda i: (layer, 0, 0)),
            pl.BlockSpec((tm, d), lambda i: (i, 0)),
            pl.BlockSpec((None, 1, d), lambda i: (ln_idx, 0, 0)),
            pl.BlockSpec((None, 1, d), lambda i: (ln_idx, 0, 0)),
        ],
        out_specs=pl.BlockSpec((tm, d), lambda i: (i, 0)),
        out_shape=jax.ShapeDtypeStruct((m, d), F32),
        compiler_params=_cparams("parallel"),
        name="outproj_ln",
    )(o, w, x, ln_g, ln_b)


def _cast_pad_cols_kernel(x_ref, o_ref):
    f = x_ref.shape[1]
    o_ref[:, :f] = x_ref[...].astype(o_ref.dtype)
    if o_ref.shape[1] > f:
        o_ref[:, f:] = jnp.zeros((o_ref.shape[0], o_ref.shape[1] - f), o_ref.dtype)


def _cast_pad_cols(w, half, f, fp):
    n, d, _ = w.shape
    rb = _pick_tile(d, (256, 128, 64, 32, 16))
    return pl.pallas_call(
        _cast_pad_cols_kernel,
        grid=(n, d // rb),
        in_specs=[pl.BlockSpec((None, rb, f), lambda l, r: (l, r, half))],
        out_specs=pl.BlockSpec((None, rb, fp), lambda l, r: (l, r, 0)),
        out_shape=jax.ShapeDtypeStruct((n, d, fp), BF16),
        compiler_params=_cparams("parallel", "parallel"),
        name="cast_pad_cols",
    )(w)


def _cast_pad_rows_kernel(x_ref, o_ref, *, valid):
    rb = x_ref.shape[0]
    row = pl.program_id(1) * rb + lax.broadcasted_iota(jnp.int32, x_ref.shape, 0)
    o_ref[...] = jnp.where(row < valid, x_ref[...], 0.0).astype(o_ref.dtype)


def _cast_pad_rows(w, fp):
    n, f, d = w.shape
    rb = math.gcd(f, fp)
    assert rb % 16 == 0
    last = f // rb - 1
    return pl.pallas_call(
        functools.partial(_cast_pad_rows_kernel, valid=f),
        grid=(n, fp // rb),
        in_specs=[pl.BlockSpec((None, rb, d), lambda l, r: (l, jnp.minimum(r, last), 0))],
        out_specs=pl.BlockSpec((None, rb, d), lambda l, r: (l, r, 0)),
        out_shape=jax.ShapeDtypeStruct((n, fp, d), BF16),
        compiler_params=_cparams("parallel", "parallel"),
        name="cast_pad_rows",
    )(w)


def _prep_ffn(w_up, w_down, tf):
    f = w_down.shape[1]
    fp = _round_up(f, tf)
    assert f % LANES == 0
    return _cast_pad_cols(w_up, 0, f, fp), _cast_pad_cols(w_up, 1, f, fp), _cast_pad_rows(w_down, fp)


def _cumsum_heads(logf_bsh):
    b, s, h = logf_bsh.shape
    sp = _round_up(s, LANES)
    x = jnp.transpose(logf_bsh, (0, 2, 1)).reshape(b * h, s)
    x = jnp.pad(x, ((0, 0), (0, sp - s)))
    return _cumsum_lanes(x)[:, :s]


def kernel(x_prompt, x_sample, cache_fox_k, cache_fox_v, cache_fox_logf, state_hgrn, ln_g, ln_b, ffn1_up, ffn1_down, ffn2_up, ffn2_down, fox_w_in, fox_b_f, fox_w_out, hgrn_w_in, hgrn_lb, hgrn_norm_g, hgrn_w_out):
    bp, tp, d = x_prompt.shape
    bs, ts, _ = x_sample.shape
    depth = ln_g.shape[0]
    heads = fox_b_f.shape[1]
    n_fox = fox_w_in.shape[0]
    past = cache_fox_k.shape[2]
    assert d == heads * LANES and hgrn_norm_g.shape[1] == LANES
    alpha = (2 * depth) ** 0.25

    tf = 512 if ffn1_down.shape[1] >= 512 else LANES
    ffn1 = _prep_ffn(ffn1_up, ffn1_down, tf)
    ffn2 = _prep_ffn(ffn2_up, ffn2_down, tf)
    fox_qkv = fox_w_in[..., :3 * d].astype(BF16)
    fox_wf = jnp.pad(fox_w_in[..., 3 * d:], ((0, 0), (0, 0), (0, LANES - heads))).astype(BF16)
    fox_bf = jnp.pad(fox_b_f, ((0, 0), (0, LANES - heads)))[:, None, :]
    fox_out = fox_w_out.astype(BF16)
    hgrn_in = hgrn_w_in.astype(BF16)
    hgrn_out = hgrn_w_out.astype(BF16)
    hgrn_ng = hgrn_norm_g[:, None, :]
    g3 = ln_g.reshape(depth * 3, 1, d)
    b3 = ln_b.reshape(depth * 3, 1, d)

    xp = x_prompt.reshape(bp * tp, d)
    xs = x_sample.reshape(bs * ts, d)
    kv_p, kv_s = None, None
    logf_p, logf_s, hg_p, hg_s = [], [], [], []

    for i in range(depth):
        j = i // 2
        xp, xpb = _ffn_ln(xp, ffn1, g3, b3, i, 3 * i, alpha, tf, with_bf16=True)
        xs, xsb = _ffn_ln(xs, ffn1, g3, b3, i, 3 * i, alpha, tf, with_bf16=True)
        if i % 2 == 0:
            q, kb, vb, k_st, v_st, lf = _fox_proj(xpb, fox_qkv, fox_wf, fox_bf, j, n_fox, heads, kv_p)
            kv_p = (k_st, v_st)
            logf = lf[:, :heads].reshape(bp, tp, heads)
            c_row = _cumsum_heads(logf)
            c_col = jnp.transpose(c_row.reshape(bp, heads, tp), (0, 2, 1)).reshape(bp * tp, heads)
            op = _fox_attn(q, kb, vb, c_row, c_col, bp, tp, heads)
            logf_p.append(logf)
            q, kb, vb, k_st, v_st, lf = _fox_proj(xsb, fox_qkv, fox_wf, fox_bf, j, n_fox, heads, kv_s)
            kv_s = (k_st, v_st)
            logf = lf[:, :heads].reshape(bs, ts, heads)
            c_all = _cumsum_heads(jnp.concatenate([cache_fox_logf[j].astype(F32), logf], axis=1))
            c_new = c_all[:, past:]
            c_col = jnp.transpose(c_new.reshape(bs, heads, ts), (0, 2, 1)).reshape(bs * ts, heads)
            os_ = _fox_attn_sample(q, kb, vb, cache_fox_k.reshape(n_fox, bs * past * heads, LANES),
                                   cache_fox_v.reshape(n_fox, bs * past * heads, LANES), j,
                                   c_all[:, :past], c_new, c_col, bs, ts, past, heads)
            logf_s.append(logf)
            w_out = fox_out
        else:
            pp = _proj(xpb, hgrn_in, j, 4 * d)
            op, sp = _hgrn(pp, hgrn_lb, hgrn_ng, None, bp, tp, heads, i, j)
            ps = _proj(xsb, hgrn_in, j, 4 * d)
            os_, ss = _hgrn(ps, hgrn_lb, hgrn_ng, state_hgrn[j], bs, ts, heads, i, j)
            hg_p.append(sp)
            hg_s.append(ss)
            w_out = hgrn_out
        xp = _outproj_ln(op, w_out, xp, g3, b3, j, 3 * i + 1, alpha)
        xs = _outproj_ln(os_, w_out, xs, g3, b3, j, 3 * i + 1, alpha)
        xp = _ffn_ln(xp, ffn2, g3, b3, i, 3 * i + 2, alpha, tf)
        xs = _ffn_ln(xs, ffn2, g3, b3, i, 3 * i + 2, alpha, tf)

    return (xp.reshape(bp, tp, d), xs.reshape(bs, ts, d),
            kv_p[0].reshape(n_fox, bp, tp, heads, LANES), kv_p[1].reshape(n_fox, bp, tp, heads, LANES),
            jnp.stack(logf_p), jnp.stack(hg_p),
            kv_s[0].reshape(n_fox, bs, ts, heads, LANES), kv_s[1].reshape(n_fox, bs, ts, heads, LANES),
            jnp.stack(logf_s), jnp.stack(hg_s))
```

```python
import functools
import math

import jax
import jax.numpy as jnp
from jax import lax
from jax.experimental import pallas as pl
from jax.experimental.pallas import tpu as pltpu

F32 = jnp.float32
BF16 = jnp.bfloat16

LN_EPS = 1e-5
RMS_EPS = 1e-6
NEG_INF = -1e30

LANES = 128
HGRN_SUB = 16
HGRN_CHUNK = 128
VMEM_LIMIT_BYTES = 48 * 1024 * 1024

def _round_up(n, m):
    return (n + m - 1) // m * m


def _pick_tile(n, candidates):
    for c in candidates:
        if n % c == 0:
            return c
    return n


def _cparams(*semantics):
    return pltpu.CompilerParams(dimension_semantics=semantics,
                                vmem_limit_bytes=VMEM_LIMIT_BYTES)


def _dot(a, b):
    return jnp.dot(a, b, preferred_element_type=F32)


def _dot_nt(a, b):
    return lax.dot_general(a, b, (((1,), (1,)), ((), ())), preferred_element_type=F32)


def _dot_tn(a, b):
    return lax.dot_general(a, b, (((0,), (0,)), ((), ())), preferred_element_type=F32)


def _sigmoid(x):
    return 1.0 / (1.0 + jnp.exp(-x))


def _log_sigmoid(x):
    return jnp.minimum(x, 0.0) - jnp.log(1.0 + jnp.exp(-jnp.abs(x)))


def _layer_norm(r, g, b):
    mu = jnp.mean(r, axis=-1, keepdims=True)
    c = r - mu
    var = jnp.mean(c * c, axis=-1, keepdims=True)
    return c * lax.rsqrt(var + LN_EPS) * g + b


def _ffn_ln_kernel(x_ref, wg_ref, wu_ref, wd_ref, g_ref, b_ref, *rest, alpha, n_tiles, ln_rows):
    out_refs, (xb_ref, acc_ref) = rest[:-2], rest[-2:]
    i = pl.program_id(0)
    f = pl.program_id(1)
    slot = lax.rem(i, 2)
    tm = xb_ref.shape[0]

    def norm_rows(src_slot):
        start = pl.multiple_of(jnp.minimum(f * ln_rows, tm - ln_rows), 16)
        rows = pl.ds(start, ln_rows)
        y = _layer_norm(0.5 * acc_ref[src_slot, rows, :], g_ref[...], b_ref[...])
        for o_ref in out_refs:
            o_ref[rows, :] = y.astype(o_ref.dtype)

    @pl.when((i == 0) & (f == 0))
    def _():
        acc_ref[1] = jnp.zeros(acc_ref.shape[1:], F32)

    @pl.when(i < n_tiles)
    def _():
        @pl.when(f == 0)
        def _():
            x = x_ref[...]
            xb_ref[...] = x.astype(BF16)
            acc_ref[slot] = (2.0 * alpha) * x

        norm_rows(1 - slot)
        xb = xb_ref[...]
        gate = _dot(xb, wg_ref[...])
        up = _dot(xb, wu_ref[...])
        h = gate * _sigmoid(gate) * up
        acc_ref[slot] += _dot(h.astype(BF16), wd_ref[...])

    @pl.when(i == n_tiles)
    def _():
        norm_rows(1 - slot)


def _ffn_ln(x, weights, ln_g, ln_b, layer, ln_idx, alpha, tf, with_bf16=False):
    w_gate, w_up, w_down = weights
    m, d = x.shape
    fp = w_down.shape[1]
    nf = fp // tf
    tm = _pick_tile(m, (512, 256, 128, 64, 32, 16))
    nt = m // tm
    ln_rows = min(tm, _round_up(-(-tm // nf), 16))
    out_dtypes = (F32, BF16) if with_bf16 else (F32,)

    def w_step(i, f):
        return jnp.where(i < nt, f, nf - 1)

    outs = pl.pallas_call(
        functools.partial(_ffn_ln_kernel, alpha=alpha, n_tiles=nt, ln_rows=ln_rows),
        grid=(nt + 1, nf),
        in_specs=[
            pl.BlockSpec((tm, d), lambda i, f: (jnp.minimum(i, nt - 1), 0)),
            pl.BlockSpec((None, d, tf), lambda i, f: (layer, 0, w_step(i, f))),
            pl.BlockSpec((None, d, tf), lambda i, f: (layer, 0, w_step(i, f))),
            pl.BlockSpec((None, tf, d), lambda i, f: (layer, w_step(i, f), 0)),
            pl.BlockSpec((None, 1, d), lambda i, f: (ln_idx, 0, 0)),
            pl.BlockSpec((None, 1, d), lambda i, f: (ln_idx, 0, 0)),
        ],
        out_specs=[pl.BlockSpec((tm, d), lambda i, f: (jnp.maximum(i - 1, 0), 0)) for _ in out_dtypes],
        out_shape=[jax.ShapeDtypeStruct((m, d), dt) for dt in out_dtypes],
        scratch_shapes=[pltpu.VMEM((tm, d), BF16), pltpu.VMEM((2, tm, d), F32)],
        compiler_params=_cparams("arbitrary", "arbitrary"),
        name="ffn_ln",
    )(x, w_gate, w_up, w_down, ln_g, ln_b)
    return outs if with_bf16 else outs[0]


def _proj_kernel(x_ref, w_ref, o_ref):
    o_ref[...] = _dot(x_ref[...], w_ref[...]).astype(o_ref.dtype)


def _proj(xb, w, layer, n, out_dtype=F32):
    m, d = xb.shape
    tm = _pick_tile(m, (2048, 1024, 512, 256, 128, 64, 32, 16))
    tn = _pick_tile(n, (512, 256, 128))
    return pl.pallas_call(
        _proj_kernel,
        grid=(m // tm, n // tn),
        in_specs=[
            pl.BlockSpec((tm, d), lambda i, j: (i, 0)),
            pl.BlockSpec((None, d, tn), lambda i, j: (layer, 0, j)),
        ],
        out_specs=pl.BlockSpec((tm, tn), lambda i, j: (i, j)),
        out_shape=jax.ShapeDtypeStruct((m, n), out_dtype),
        compiler_params=_cparams("parallel", "arbitrary"),
        name="proj",
    )(xb, w)


def _fox_proj_kernel(x_ref, w_ref, wf_ref, bf_ref, *rest, nq, heads):
    q_ref, kb_ref, vb_ref, k_ref, v_ref, lf_ref = rest[-6:]
    c = pl.program_id(0)
    tm, tn = q_ref.shape

    def cache_rows(o_ref, y):
        o_ref[...] = y.reshape(tm, tn // LANES, LANES)

    @pl.when(c < nq)
    def _():
        q_ref[...] = _dot(x_ref[...], w_ref[...]).astype(q_ref.dtype)

    @pl.when((c >= nq) & (c < 2 * nq))
    def _():
        y = _dot(x_ref[...], w_ref[...])
        kb_ref[...] = y.astype(kb_ref.dtype)
        cache_rows(k_ref, y)

    @pl.when((c >= 2 * nq) & (c < 3 * nq))
    def _():
        y = _dot(x_ref[...], w_ref[...])
        vb_ref[...] = y.astype(vb_ref.dtype)
        cache_rows(v_ref, y)

    @pl.when(c == 3 * nq)
    def _():
        lf_ref[...] = _log_sigmoid(_dot(x_ref[...], wf_ref[...]) + bf_ref[...])


def _fox_proj(xb, w_qkv, wf, bf, layer, n_layers, heads, kv_stacks):
    m, d = xb.shape
    tm = _pick_tile(m, (512, 256, 128, 64, 32, 16))
    tn = 1024 if d % 1024 == 0 else d
    nq = d // tn
    hpt = tn // LANES
    assert hpt % 8 == 0 or hpt == heads
    nt = m // tm

    def rows(c, i, lo):
        return jnp.where(c < lo, 0, jnp.where(c < lo + nq, i, nt - 1))

    def cols(c, lo):
        return jnp.clip(c - lo, 0, nq - 1)

    in_specs = [
        pl.BlockSpec((tm, d), lambda c, i: (i, 0)),
        pl.BlockSpec((None, d, tn), lambda c, i: (layer, 0, jnp.minimum(c, 3 * nq - 1))),
        pl.BlockSpec((None, d, LANES), lambda c, i: (layer, 0, 0)),
        pl.BlockSpec((None, 1, LANES), lambda c, i: (layer, 0, 0)),
    ]
    args = [xb, w_qkv, wf, bf]
    aliases = {}
    if kv_stacks is not None:
        in_specs += [pl.BlockSpec(memory_space=pl.ANY), pl.BlockSpec(memory_space=pl.ANY)]
        args += list(kv_stacks)
        aliases = {4: 3, 5: 4}
    return pl.pallas_call(
        functools.partial(_fox_proj_kernel, nq=nq, heads=heads),
        grid=(3 * nq + 1, nt),
        in_specs=in_specs,
        out_specs=[
            pl.BlockSpec((tm, tn), lambda c, i: (rows(c, i, 0), cols(c, 0))),
            pl.BlockSpec((tm, tn), lambda c, i: (rows(c, i, nq), cols(c, nq))),
            pl.BlockSpec((tm, tn), lambda c, i: (rows(c, i, 2 * nq), cols(c, 2 * nq))),
            pl.BlockSpec((None, tm, hpt, LANES), lambda c, i: (layer, rows(c, i, nq), cols(c, nq), 0)),
            pl.BlockSpec((None, tm, hpt, LANES), lambda c, i: (layer, rows(c, i, 2 * nq), cols(c, 2 * nq), 0)),
            pl.BlockSpec((tm, LANES), lambda c, i: (jnp.where(c == 3 * nq, i, 0), 0)),
        ],
        out_shape=[
            jax.ShapeDtypeStruct((m, d), BF16),
            jax.ShapeDtypeStruct((m, d), BF16),
            jax.ShapeDtypeStruct((m, d), BF16),
            jax.ShapeDtypeStruct((n_layers, m, heads, LANES), F32),
            jax.ShapeDtypeStruct((n_layers, m, heads, LANES), F32),
            jax.ShapeDtypeStruct((m, LANES), F32),
        ],
        input_output_aliases=aliases,
        compiler_params=_cparams("arbitrary", "arbitrary"),
        name="fox_proj",
    )(*args)


def _cumsum_lanes_kernel(x_ref, o_ref):
    x = x_ref[...]
    n = x.shape[-1]
    lane = lax.broadcasted_iota(jnp.int32, x.shape, 1)
    s = 1
    while s < n:
        x = x + jnp.where(lane >= s, pltpu.roll(x, s, 1), 0.0)
        s *= 2
    o_ref[...] = x


def _cumsum_lanes(x):
    r, n = x.shape
    rb = _pick_tile(r, (64, 32, 16, 8))
    return pl.pallas_call(
        _cumsum_lanes_kernel,
        grid=(r // rb,),
        in_specs=[pl.BlockSpec((rb, n), lambda i: (i, 0))],
        out_specs=pl.BlockSpec((rb, n), lambda i: (i, 0)),
        out_shape=jax.ShapeDtypeStruct((r, n), F32),
        compiler_params=_cparams("parallel"),
        name="cumsum_lanes",
    )(x)


def _head_column(c, h):
    lane = lax.broadcasted_iota(jnp.int32, c.shape, 1)
    return jnp.sum(jnp.where(lane == h, c, 0.0), axis=1, keepdims=True)


_LOG2E = math.log2(math.e)


def _fox_attn_kernel(q_ref, kb_ref, vb_ref, crow_ref, ccol_ref, o_ref, vext_ref, *, tq, scale, hp):
    g = pl.program_id(1)
    a_coef = scale * _LOG2E
    inv_scale = 1.0 / scale
    seq = q_ref.shape[0]
    for hh in range(hp):
        vext_ref[:, 2 * hh * LANES:(2 * hh + 1) * LANES] = vb_ref[:, hh * LANES:(hh + 1) * LANES]
        vext_ref[:, (2 * hh + 1) * LANES:2 * (hh + 1) * LANES] = jnp.ones((seq, LANES), BF16)
    cc = ccol_ref[...]
    row = lax.broadcasted_iota(jnp.int32, (tq, tq), 0)
    col = lax.broadcasted_iota(jnp.int32, (tq, tq), 1)
    ci_all = [_head_column(cc, g * hp + hh) * inv_scale for hh in range(hp)]
    cj_all = [crow_ref[hh] * inv_scale for hh in range(hp)]
    starts = list(range(0, seq, tq))
    pairs = [starts[i:i + 1] + starts[len(starts) - 1 - i:len(starts) - i]
             for i in range(len(starts) // 2)] or [starts]
    for pair in pairs:
        items = [(hh, r0) for r0 in pair for hh in range(hp)]
        sls = {hh: slice(hh * LANES, (hh + 1) * LANES) for hh in range(hp)}
        u = {}
        for (hh, r0) in items:
            w = r0 + tq
            u[hh, r0] = _dot_nt(q_ref[r0:w, sls[hh]], kb_ref[0:w, sls[hh]]) - cj_all[hh][:, 0:w]
        r = {}
        for (hh, r0) in items:
            w = r0 + tq
            diag = jnp.where(col <= row, u[hh, r0][:, r0:w], NEG_INF)
            u[hh, r0] = diag if r0 == 0 else jnp.concatenate([u[hh, r0][:, 0:r0], diag], axis=1)
            ci = ci_all[hh][r0:w, :]
            m = jnp.max(u[hh, r0], axis=1, keepdims=True) + ci
            r[hh, r0] = m - ci
        p = {t: jnp.exp2(a_coef * (u[t] - r[t])).astype(BF16) for t in items}
        for (hh, r0) in items:
            w = r0 + tq
            ov = _dot(p[hh, r0], vext_ref[0:w, 2 * hh * LANES:2 * (hh + 1) * LANES])
            o_ref[r0:w, sls[hh]] = (ov[:, :LANES] / ov[:, LANES:]).astype(o_ref.dtype)


def _fox_attn(q, kb, vb, c_row, c_col, batch, seq, heads):
    m, d = q.shape
    hp = 2 if heads % 2 == 0 else 1
    tq = _pick_tile(seq, (256, 128))
    ng = heads // hp
    w = hp * LANES
    c_row = c_row.reshape(batch * ng, hp, 1, seq)
    return pl.pallas_call(
        functools.partial(_fox_attn_kernel, tq=tq, scale=LANES ** -0.5, hp=hp),
        grid=(batch, ng),
        in_specs=[
            pl.BlockSpec((seq, w), lambda b, g: (b, g)),
            pl.BlockSpec((seq, w), lambda b, g: (b, g)),
            pl.BlockSpec((seq, w), lambda b, g: (b, g)),
            pl.BlockSpec((None, hp, 1, seq), lambda b, g: (b * ng + g, 0, 0, 0)),
            pl.BlockSpec((seq, heads), lambda b, g: (b, 0)),
        ],
        out_specs=pl.BlockSpec((seq, w), lambda b, g: (b, g)),
        out_shape=jax.ShapeDtypeStruct((m, d), BF16),
        scratch_shapes=[pltpu.VMEM((seq, 2 * w), BF16)],
        compiler_params=_cparams("parallel", "parallel"),
        name="fox_attn",
    )(q, kb, vb, c_row, c_col)


def _fox_attn_sample_kernel(q_ref, kn_ref, vn_ref, kp_ref, vp_ref, cpast_ref, cnew_ref, ccol_ref,
                            o_ref, *, scale, heads):
    a_coef = scale * _LOG2E
    inv_scale = 1.0 / scale
    ts = q_ref.shape[0]
    past = kp_ref.shape[0] // heads
    cc = ccol_ref[...]
    row = lax.broadcasted_iota(jnp.int32, (ts, ts), 0)
    col = lax.broadcasted_iota(jnp.int32, (ts, ts), 1)
    group = _pick_tile(heads, (4, 2, 1))
    for h0 in range(0, heads, group):
        hs = list(range(h0, h0 + group))
        sls = {h: slice(h * LANES, (h + 1) * LANES) for h in hs}
        kp = {h: kp_ref[pl.ds(h, past, stride=heads), :].astype(BF16) for h in hs}
        vp = {h: vp_ref[pl.ds(h, past, stride=heads), :].astype(BF16) for h in hs}
        up = {h: _dot_nt(q_ref[:, sls[h]], kp[h]) - cpast_ref[h] * inv_scale for h in hs}
        un = {h: jnp.where(col <= row, _dot_nt(q_ref[:, sls[h]], kn_ref[:, sls[h]]) - cnew_ref[h] * inv_scale,
                           NEG_INF) for h in hs}
        r = {}
        for h in hs:
            ci = _head_column(cc, h) * inv_scale
            m = jnp.maximum(jnp.max(up[h], axis=1, keepdims=True), jnp.max(un[h], axis=1, keepdims=True)) + ci
            r[h] = m - ci
        for h in hs:
            pp = jnp.exp2(a_coef * (up[h] - r[h]))
            pn = jnp.exp2(a_coef * (un[h] - r[h]))
            l = jnp.sum(pp, axis=1, keepdims=True) + jnp.sum(pn, axis=1, keepdims=True)
            o = (_dot(pp.astype(BF16), vp[h]) + _dot(pn.astype(BF16), vn_ref[:, sls[h]])) / l
            o_ref[:, sls[h]] = o.astype(o_ref.dtype)


def _fox_attn_sample(q, kb, vb, k_past, v_past, layer, c_past, c_new, c_col, batch, ts, past, heads):
    m, d = q.shape
    c_past = c_past.reshape(batch, heads, 1, past)
    c_new = c_new.reshape(batch, heads, 1, ts)
    return pl.pallas_call(
        functools.partial(_fox_attn_sample_kernel, scale=LANES ** -0.5, heads=heads),
        grid=(batch,),
        in_specs=[
            pl.BlockSpec((ts, d), lambda b: (b, 0)),
            pl.BlockSpec((ts, d), lambda b: (b, 0)),
            pl.BlockSpec((ts, d), lambda b: (b, 0)),
            pl.BlockSpec((None, past * heads, LANES), lambda b: (layer, b, 0)),
            pl.BlockSpec((None, past * heads, LANES), lambda b: (layer, b, 0)),
            pl.BlockSpec((None, heads, 1, past), lambda b: (b, 0, 0, 0)),
            pl.BlockSpec((None, heads, 1, ts), lambda b: (b, 0, 0, 0)),
            pl.BlockSpec((ts, heads), lambda b: (b, 0)),
        ],
        out_specs=pl.BlockSpec((ts, d), lambda b: (b, 0)),
        out_shape=jax.ShapeDtypeStruct((m, d), BF16),
        compiler_params=_cparams("parallel"),
        name="fox_attn_sample",
    )(q, kb, vb, k_past, v_past, c_past, c_new, c_col)


def _expand_rows(f, reps):
    parts = [jnp.broadcast_to(f[r:r + 1, :], (reps, f.shape[1])) for r in range(f.shape[0])]
    return parts[0] if len(parts) == 1 else jnp.concatenate(parts, axis=0)


def _take_rows(f, idx):
    parts = [f[r:r + 1, :] for r in idx]
    return parts[0] if len(parts) == 1 else jnp.concatenate(parts, axis=0)


def _hgrn_group(q, z, v, go, states, bscr_ref, lb, ng, kscale, n):
    n_g = q.shape[0] // n
    hp = q.shape[1] // LANES
    nblk = n // HGRN_SUB
    tiles = [(u, hh) for u in range(n_g) for hh in range(hp)]

    def tile(x, u, hh):
        return x[u * n:(u + 1) * n, hh * LANES:(hh + 1) * LANES]

    e = jnp.exp(-jnp.abs(z))
    r1 = 1.0 / (1.0 + e)
    er1 = e * r1
    pos = z > 0.0
    g = jnp.log(lb + (1.0 - lb) * jnp.where(pos, r1, er1))
    kk = (1.0 - lb) * jnp.where(pos, er1, r1)
    qf = q * _sigmoid(q) * kscale
    gate = go * _sigmoid(go)
    vb = v.astype(BF16)

    row = lax.broadcasted_iota(jnp.int32, (n, n), 0)
    col = lax.broadcasted_iota(jnp.int32, (n, n), 1)
    tri = jnp.where(col <= row, 1.0, 0.0).astype(BF16)
    g_hi = g.astype(BF16)
    g_r1 = g - g_hi.astype(F32)
    g_mid = g_r1.astype(BF16)
    g_lo = (g_r1 - g_mid.astype(F32)).astype(BF16)
    b = {t: _dot(tri, tile(g_hi, *t)) + _dot(tri, tile(g_mid, *t)) + _dot(tri, tile(g_lo, *t))
         for t in tiles}
    for (u, hh) in tiles:
        bscr_ref[hh, u] = b[u, hh]
    bmid = {(u, hh): bscr_ref[hh, u, pl.ds(HGRN_SUB // 2, nblk, stride=HGRN_SUB), :] for (u, hh) in tiles}
    bend = {(u, hh): bscr_ref[hh, u, pl.ds(HGRN_SUB - 1, nblk, stride=HGRN_SUB), :] for (u, hh) in tiles}
    blast = {t: bend[t][nblk - 1:nblk, :] for t in tiles}
    qd, kd = {}, {}
    for t in tiles:
        bmid_rows = _expand_rows(bmid[t], HGRN_SUB)
        qd[t] = tile(qf, *t) * jnp.exp(b[t] - bmid_rows)
        kd[t] = tile(kk, *t) * jnp.exp(bmid_rows - b[t])

    blk = lax.broadcasted_iota(jnp.int32, (nblk, LANES), 0)
    a = {t: None for t in tiles}
    s = nblk // 2
    while s >= 1:
        upper = ((blk >> (s.bit_length() - 1)) & 1) == 1
        group_shift = (2 * s * HGRN_SUB).bit_length() - 1
        same_group = (row >> group_shift) == (col >> group_shift)
        for t in tiles:
            ref = _take_rows(bend[t], [(i // (2 * s)) * 2 * s + s - 1 for i in range(nblk)])
            fq = jnp.where(upper, jnp.exp(jnp.minimum(bmid[t] - ref, 0.0)), 0.0)
            fk = jnp.where(upper, 0.0, jnp.exp(jnp.minimum(ref - bmid[t], 0.0)))
            ql = qd[t] * _expand_rows(fq, HGRN_SUB)
            kl = kd[t] * _expand_rows(fk, HGRN_SUB)
            prod = _dot_nt(ql.astype(BF16), kl.astype(BF16))
            a[t] = prod if a[t] is None else jnp.where(same_group, prod, a[t])
        s //= 2
    sub_shift = HGRN_SUB.bit_length() - 1
    on_diag = ((row >> sub_shift) == (col >> sub_shift)) & (col <= row)
    o_in, q_in, upd, decay = {}, {}, {}, {}
    for t in tiles:
        pd = _dot_nt(qd[t].astype(BF16), kd[t].astype(BF16))
        at = jnp.where(on_diag, pd, 0.0 if a[t] is None else a[t])
        vt = tile(vb, *t)
        o_in[t] = _dot(at.astype(BF16), vt)
        q_in[t] = (qd[t] * _expand_rows(jnp.exp(bmid[t]), HGRN_SUB)).astype(BF16)
        k_out = kd[t] * _expand_rows(jnp.exp(blast[t] - bmid[t]), HGRN_SUB)
        upd[t] = _dot_tn(vt, k_out.astype(BF16))
        decay[t] = jnp.exp(blast[t])

    states = list(states)
    rows_out = []
    for u in range(n_g):
        cols_out = []
        for hh in range(hp):
            t = (u, hh)
            o = o_in[t] + _dot_nt(q_in[t], states[hh].astype(BF16))
            states[hh] = states[hh] * decay[t] + upd[t]
            o = o * lax.rsqrt(jnp.mean(o * o, axis=-1, keepdims=True) + RMS_EPS) * ng
            cols_out.append(o * tile(gate, *t))
        rows_out.append(cols_out[0] if hp == 1 else jnp.concatenate(cols_out, axis=1))
    out = rows_out[0] if n_g == 1 else jnp.concatenate(rows_out, axis=0)
    return out, states


def _hgrn_kernel(*refs, chunk, n_chunks, has_s0, layer, kscale, hp):
    if has_s0:
        q_ref, z_ref, i_ref, g_ref, lbp_ref, ng_ref, s0_ref, o_ref, s_ref, bscr_ref = refs
    else:
        q_ref, z_ref, i_ref, g_ref, lbp_ref, ng_ref, o_ref, s_ref, bscr_ref = refs
        s0_ref = None

    lbp = lbp_ref[...]
    e = jnp.exp(lbp - jnp.max(lbp, axis=0, keepdims=True))
    soft = e / jnp.sum(e, axis=0, keepdims=True)
    r = lax.broadcasted_iota(jnp.int32, lbp.shape, 0)
    lb_all = jnp.sum(jnp.where((r >= 1) & (r <= layer), soft, 0.0), axis=0, keepdims=True)
    ng = ng_ref[...]

    group = bscr_ref.shape[1]

    def step(t, states):
        rows = pl.ds(pl.multiple_of(t * group * chunk, group * chunk), group * chunk)
        o, states = _hgrn_group(q_ref[rows, :], z_ref[rows, :], i_ref[rows, :], g_ref[rows, :],
                                states, bscr_ref, lb_all, ng, kscale, chunk)
        o_ref[rows, :] = o.astype(o_ref.dtype)
        return tuple(states)

    if has_s0:
        init = tuple(s0_ref[hh].T for hh in range(hp))
    else:
        init = tuple(jnp.zeros((LANES, LANES), F32) for _ in range(hp))
    if n_chunks == group:
        states = step(0, init)
    else:
        states = lax.fori_loop(0, n_chunks // group, step, init)
    for hh in range(hp):
        s_ref[hh] = states[hh].T


def _hgrn(proj, lb_param, norm_g, s0, batch, seq, heads, layer, mixer_idx):
    m = proj.shape[0]
    d = heads * LANES
    chunk = _pick_tile(seq, (HGRN_CHUNK, 64, 32, 16))
    assert chunk % HGRN_SUB == 0 and seq % chunk == 0
    depth = lb_param.shape[0]
    has_s0 = s0 is not None
    hp = _pick_tile(heads, (2, 1) if seq > chunk else (4, 2, 1))
    group = _pick_tile(seq // chunk, (4, 2, 1))
    ng = heads // hp
    w = hp * LANES
    in_specs = [
        pl.BlockSpec((seq, w), lambda b, g: (b, g)),
        pl.BlockSpec((seq, w), lambda b, g: (b, ng + g)),
        pl.BlockSpec((seq, w), lambda b, g: (b, 2 * ng + g)),
        pl.BlockSpec((seq, w), lambda b, g: (b, 3 * ng + g)),
        pl.BlockSpec((depth, w), lambda b, g: (0, g)),
        pl.BlockSpec((None, 1, LANES), lambda b, g: (mixer_idx, 0, 0)),
    ]
    args = [proj, proj, proj, proj, lb_param, norm_g]
    if has_s0:
        in_specs.append(pl.BlockSpec((None, hp, LANES, LANES), lambda b, g: (b, g, 0, 0)))
        args.append(s0)
    return pl.pallas_call(
        functools.partial(_hgrn_kernel, chunk=chunk, n_chunks=seq // chunk, has_s0=has_s0,
                          layer=layer, kscale=LANES ** -0.5, hp=hp),
        grid=(batch, ng),
        in_specs=in_specs,
        out_specs=[
            pl.BlockSpec((seq, w), lambda b, g: (b, g)),
            pl.BlockSpec((None, hp, LANES, LANES), lambda b, g: (b, g, 0, 0)),
        ],
        out_shape=[
            jax.ShapeDtypeStruct((m, d), BF16),
            jax.ShapeDtypeStruct((batch, heads, LANES, LANES), F32),
        ],
        scratch_shapes=[pltpu.VMEM((hp, group, chunk, LANES), F32)],
        compiler_params=_cparams("parallel", "parallel"),
        name="hgrn",
    )(*args)


def _outproj_ln_kernel(o_ref, w_ref, x_ref, g_ref, b_ref, y_ref, *, alpha, rows):
    for r0 in range(0, o_ref.shape[0], rows):
        sl = slice(r0, r0 + rows)
        r = alpha * x_ref[sl, :] + _dot(o_ref[sl, :], w_ref[...])
        y_ref[sl, :] = _layer_norm(r, g_ref[...], b_ref[...])


def _outproj_ln(o, w, x, ln_g, ln_b, layer, ln_idx, alpha):
    m, d = x.shape
    tm = _pick_tile(m, (512, 256, 128, 64, 32, 16))
    return pl.pallas_call(
        functools.partial(_outproj_ln_kernel, alpha=alpha, rows=_pick_tile(tm, (128,))),
        grid=(m // tm,),
        in_specs=[
            pl.BlockSpec((tm, d), lambda i: (i, 0)),
            pl.BlockSpec((None, d, d), lambda i: (layer, 0, 0)),
            pl.BlockSpec((tm, d), lambda i: (i, 0)),
            pl.BlockSpec((None, 1, d), lambda i: (ln_idx, 0, 0)),
            pl.BlockSpec((None, 1, d), lambda i: (ln_idx, 0, 0)),
        ],
        out_specs=pl.BlockSpec((tm, d), lambda i: (i, 0)),
        out_shape=jax.ShapeDtypeStruct((m, d), F32),
        compiler_params=_cparams("parallel"),
        name="outproj_ln",
    )(o, w, x, ln_g, ln_b)


def _cast_pad_cols_kernel(x_ref, o_ref):
    f = x_ref.shape[1]
    o_ref[:, :f] = x_ref[...].astype(o_ref.dtype)
    if o_ref.shape[1] > f:
        o_ref[:, f:] = jnp.zeros((o_ref.shape[0], o_ref.shape[1] - f), o_ref.dtype)


def _cast_pad_cols(w, half, f, fp):
    n, d, _ = w.shape
    rb = _pick_tile(d, (256, 128, 64, 32, 16))
    return pl.pallas_call(
        _cast_pad_cols_kernel,
        grid=(n, d // rb),
        in_specs=[pl.BlockSpec((None, rb, f), lambda l, r: (l, r, half))],
        out_specs=pl.BlockSpec((None, rb, fp), lambda l, r: (l, r, 0)),
        out_shape=jax.ShapeDtypeStruct((n, d, fp), BF16),
        compiler_params=_cparams("parallel", "parallel"),
        name="cast_pad_cols",
    )(w)


def _cast_rows_kernel(x_ref, *rest, row0, valid):
    o_ref = rest[-1]
    rb = x_ref.shape[0]
    row = row0 + pl.program_id(1) * rb + lax.broadcasted_iota(jnp.int32, x_ref.shape, 0)
    o_ref[...] = jnp.where(row < valid, x_ref[...], 0.0).astype(o_ref.dtype)


def _cast_pad_rows(w, fp, tf):
    n, f, d = w.shape
    f_main = f // tf * tf
    rb = _pick_tile(f_main, (1280, 1024, 640, 512, 256, 128, 64, 32, 16))
    out = pl.pallas_call(
        functools.partial(_cast_rows_kernel, row0=0, valid=f),
        grid=(n, f_main // rb),
        in_specs=[pl.BlockSpec((None, rb, d), lambda l, r: (l, r, 0))],
        out_specs=pl.BlockSpec((None, rb, d), lambda l, r: (l, r, 0)),
        out_shape=jax.ShapeDtypeStruct((n, fp, d), BF16),
        compiler_params=_cparams("parallel", "parallel"),
        name="cast_rows",
    )(w)
    if fp == f_main:
        return out
    rb = math.gcd(math.gcd(f_main, f), fp)
    assert rb % 16 == 0
    first, last = f_main // rb, f // rb - 1
    return pl.pallas_call(
        functools.partial(_cast_rows_kernel, row0=f_main, valid=f),
        grid=(n, (fp - f_main) // rb),
        in_specs=[pl.BlockSpec((None, rb, d), lambda l, r: (l, jnp.minimum(first + r, last), 0)),
                  pl.BlockSpec(memory_space=pl.ANY)],
        out_specs=pl.BlockSpec((None, rb, d), lambda l, r: (l, first + r, 0)),
        out_shape=jax.ShapeDtypeStruct((n, fp, d), BF16),
        input_output_aliases={1: 0},
        compiler_params=_cparams("parallel", "parallel"),
        name="cast_rows_tail",
    )(w, out)


def _prep_ffn(w_up, w_down, tf):
    f = w_down.shape[1]
    fp = _round_up(f, tf)
    assert f % LANES == 0 and f >= tf
    return _cast_pad_cols(w_up, 0, f, fp), _cast_pad_cols(w_up, 1, f, fp), _cast_pad_rows(w_down, fp, tf)


def _cumsum_heads(logf_bsh):
    b, s, h = logf_bsh.shape
    sp = _round_up(s, LANES)
    x = jnp.transpose(logf_bsh, (0, 2, 1)).reshape(b * h, s)
    x = jnp.pad(x, ((0, 0), (0, sp - s)))
    return _cumsum_lanes(x)[:, :s]


def kernel(x_prompt, x_sample, cache_fox_k, cache_fox_v, cache_fox_logf, state_hgrn, ln_g, ln_b, ffn1_up, ffn1_down, ffn2_up, ffn2_down, fox_w_in, fox_b_f, fox_w_out, hgrn_w_in, hgrn_lb, hgrn_norm_g, hgrn_w_out):
    bp, tp, d = x_prompt.shape
    bs, ts, _ = x_sample.shape
    depth = ln_g.shape[0]
    heads = fox_b_f.shape[1]
    n_fox = fox_w_in.shape[0]
    past = cache_fox_k.shape[2]
    assert d == heads * LANES and hgrn_norm_g.shape[1] == LANES
    alpha = (2 * depth) ** 0.25

    tf = 512 if ffn1_down.shape[1] >= 512 else LANES
    ffn1 = _prep_ffn(ffn1_up, ffn1_down, tf)
    ffn2 = _prep_ffn(ffn2_up, ffn2_down, tf)
    fox_qkv = fox_w_in[..., :3 * d].astype(BF16)
    fox_wf = jnp.pad(fox_w_in[..., 3 * d:], ((0, 0), (0, 0), (0, LANES - heads))).astype(BF16)
    fox_bf = jnp.pad(fox_b_f, ((0, 0), (0, LANES - heads)))[:, None, :]
    fox_out = fox_w_out.astype(BF16)
    hgrn_in = hgrn_w_in.astype(BF16)
    hgrn_out = hgrn_w_out.astype(BF16)
    hgrn_ng = hgrn_norm_g[:, None, :]
    g3 = ln_g.reshape(depth * 3, 1, d)
    b3 = ln_b.reshape(depth * 3, 1, d)

    xp = x_prompt.reshape(bp * tp, d)
    xs = x_sample.reshape(bs * ts, d)
    kv_p, kv_s = None, None
    logf_p, logf_s, hg_p, hg_s = [], [], [], []

    for i in range(depth):
        j = i // 2
        xp, xpb = _ffn_ln(xp, ffn1, g3, b3, i, 3 * i, alpha, tf, with_bf16=True)
        xs, xsb = _ffn_ln(xs, ffn1, g3, b3, i, 3 * i, alpha, tf, with_bf16=True)
        if i % 2 == 0:
            q, kb, vb, k_st, v_st, lf = _fox_proj(xpb, fox_qkv, fox_wf, fox_bf, j, n_fox, heads, kv_p)
            kv_p = (k_st, v_st)
            logf = lf[:, :heads].reshape(bp, tp, heads)
            c_row = _cumsum_heads(logf)
            c_col = jnp.transpose(c_row.reshape(bp, heads, tp), (0, 2, 1)).reshape(bp * tp, heads)
            op = _fox_attn(q, kb, vb, c_row, c_col, bp, tp, heads)
            logf_p.append(logf)
            q, kb, vb, k_st, v_st, lf = _fox_proj(xsb, fox_qkv, fox_wf, fox_bf, j, n_fox, heads, kv_s)
            kv_s = (k_st, v_st)
            logf = lf[:, :heads].reshape(bs, ts, heads)
            c_all = _cumsum_heads(jnp.concatenate([cache_fox_logf[j].astype(F32), logf], axis=1))
            c_new = c_all[:, past:]
            c_col = jnp.transpose(c_new.reshape(bs, heads, ts), (0, 2, 1)).reshape(bs * ts, heads)
            os_ = _fox_attn_sample(q, kb, vb, cache_fox_k.reshape(n_fox, bs * past * heads, LANES),
                                   cache_fox_v.reshape(n_fox, bs * past * heads, LANES), j,
                                   c_all[:, :past], c_new, c_col, bs, ts, past, heads)
            logf_s.append(logf)
            w_out = fox_out
        else:
            pp = _proj(xpb, hgrn_in, j, 4 * d)
            op, sp = _hgrn(pp, hgrn_lb, hgrn_ng, None, bp, tp, heads, i, j)
            ps = _proj(xsb, hgrn_in, j, 4 * d)
            os_, ss = _hgrn(ps, hgrn_lb, hgrn_ng, state_hgrn[j], bs, ts, heads, i, j)
            hg_p.append(sp)
            hg_s.append(ss)
            w_out = hgrn_out
        xp = _outproj_ln(op, w_out, xp, g3, b3, j, 3 * i + 1, alpha)
        xs = _outproj_ln(os_, w_out, xs, g3, b3, j, 3 * i + 1, alpha)
        xp = _ffn_ln(xp, ffn2, g3, b3, i, 3 * i + 2, alpha, tf)
        xs = _ffn_ln(xs, ffn2, g3, b3, i, 3 * i + 2, alpha, tf)

    return (xp.reshape(bp, tp, d), xs.reshape(bs, ts, d),
            kv_p[0].reshape(n_fox, bp, tp, heads, LANES), kv_p[1].reshape(n_fox, bp, tp, heads, LANES),
            jnp.stack(logf_p), jnp.stack(hg_p),
            kv_s[0].reshape(n_fox, bs, ts, heads, LANES), kv_s[1].reshape(n_fox, bs, ts, heads, LANES),
            jnp.stack(logf_s), jnp.stack(hg_s))
```

```python
import functools
import math

import jax
import jax.numpy as jnp
from jax import lax
from jax.experimental import pallas as pl
from jax.experimental.pallas import tpu as pltpu

F32 = jnp.float32
BF16 = jnp.bfloat16

LN_EPS = 1e-5
RMS_EPS = 1e-6
NEG_INF = -1e30

LANES = 128
HGRN_SUB = 16
HGRN_CHUNK = 128
VMEM_LIMIT_BYTES = 48 * 1024 * 1024

def _round_up(n, m):
    return (n + m - 1) // m * m


def _pick_tile(n, candidates):
    for c in candidates:
        if n % c == 0:
            return c
    return n


def _cparams(*semantics):
    return pltpu.CompilerParams(dimension_semantics=semantics,
                                vmem_limit_bytes=VMEM_LIMIT_BYTES)


def _dot(a, b):
    return jnp.dot(a, b, preferred_element_type=F32)


def _dot_nt(a, b):
    return lax.dot_general(a, b, (((1,), (1,)), ((), ())), preferred_element_type=F32)


def _dot_tn(a, b):
    return lax.dot_general(a, b, (((0,), (0,)), ((), ())), preferred_element_type=F32)


def _sigmoid(x):
    return 1.0 / (1.0 + jnp.exp(-x))


def _log_sigmoid(x):
    return jnp.minimum(x, 0.0) - jnp.log(1.0 + jnp.exp(-jnp.abs(x)))


def _layer_norm(r, g, b):
    mu = jnp.mean(r, axis=-1, keepdims=True)
    c = r - mu
    var = jnp.mean(c * c, axis=-1, keepdims=True)
    return c * lax.rsqrt(var + LN_EPS) * g + b


def _ffn_ln_kernel(x_ref, wg_ref, wu_ref, wd_ref, g_ref, b_ref, *rest, alpha, n_tiles, ln_rows):
    out_refs, (xb_ref, acc_ref) = rest[:-2], rest[-2:]
    i = pl.program_id(0)
    f = pl.program_id(1)
    slot = lax.rem(i, 2)
    tm = xb_ref.shape[0]

    def norm_rows(src_slot):
        start = pl.multiple_of(jnp.minimum(f * ln_rows, tm - ln_rows), 16)
        rows = pl.ds(start, ln_rows)
        y = _layer_norm(0.5 * acc_ref[src_slot, rows, :], g_ref[...], b_ref[...])
        for o_ref in out_refs:
            o_ref[rows, :] = y.astype(o_ref.dtype)

    @pl.when((i == 0) & (f == 0))
    def _():
        acc_ref[1] = jnp.zeros(acc_ref.shape[1:], F32)

    @pl.when(i < n_tiles)
    def _():
        @pl.when(f == 0)
        def _():
            x = x_ref[...]
            xb_ref[...] = x.astype(BF16)
            acc_ref[slot] = (2.0 * alpha) * x

        norm_rows(1 - slot)
        xb = xb_ref[...]
        gate = _dot(xb, wg_ref[...])
        up = _dot(xb, wu_ref[...])
        h = gate * _sigmoid(gate) * up
        acc_ref[slot] += _dot(h.astype(BF16), wd_ref[...])

    @pl.when(i == n_tiles)
    def _():
        norm_rows(1 - slot)


def _ffn_ln(x, weights, ln_g, ln_b, layer, ln_idx, alpha, tf, with_bf16=False):
    w_gate, w_up, w_down = weights
    m, d = x.shape
    fp = w_down.shape[1]
    nf = fp // tf
    tm = _pick_tile(m, (512, 256, 128, 64, 32, 16))
    nt = m // tm
    ln_rows = min(tm, _round_up(-(-tm // nf), 16))
    out_dtypes = (F32, BF16) if with_bf16 else (F32,)

    def w_step(i, f):
        return jnp.where(i < nt, f, nf - 1)

    outs = pl.pallas_call(
        functools.partial(_ffn_ln_kernel, alpha=alpha, n_tiles=nt, ln_rows=ln_rows),
        grid=(nt + 1, nf),
        in_specs=[
            pl.BlockSpec((tm, d), lambda i, f: (jnp.minimum(i, nt - 1), 0)),
            pl.BlockSpec((None, d, tf), lambda i, f: (layer, 0, w_step(i, f))),
            pl.BlockSpec((None, d, tf), lambda i, f: (layer, 0, w_step(i, f))),
            pl.BlockSpec((None, tf, d), lambda i, f: (layer, w_step(i, f), 0)),
            pl.BlockSpec((None, 1, d), lambda i, f: (ln_idx, 0, 0)),
            pl.BlockSpec((None, 1, d), lambda i, f: (ln_idx, 0, 0)),
        ],
        out_specs=[pl.BlockSpec((tm, d), lambda i, f: (jnp.maximum(i - 1, 0), 0)) for _ in out_dtypes],
        out_shape=[jax.ShapeDtypeStruct((m, d), dt) for dt in out_dtypes],
        scratch_shapes=[pltpu.VMEM((tm, d), BF16), pltpu.VMEM((2, tm, d), F32)],
        compiler_params=_cparams("arbitrary", "arbitrary"),
        name="ffn_ln",
    )(x, w_gate, w_up, w_down, ln_g, ln_b)
    return outs if with_bf16 else outs[0]


def _proj_kernel(x_ref, w_ref, o_ref):
    o_ref[...] = _dot(x_ref[...], w_ref[...]).astype(o_ref.dtype)


def _proj(xb, w, layer, n, out_dtype=F32):
    m, d = xb.shape
    tm = _pick_tile(m, (2048, 1024, 512, 256, 128, 64, 32, 16))
    tn = _pick_tile(n, (512, 256, 128))
    return pl.pallas_call(
        _proj_kernel,
        grid=(m // tm, n // tn),
        in_specs=[
            pl.BlockSpec((tm, d), lambda i, j: (i, 0)),
            pl.BlockSpec((None, d, tn), lambda i, j: (layer, 0, j)),
        ],
        out_specs=pl.BlockSpec((tm, tn), lambda i, j: (i, j)),
        out_shape=jax.ShapeDtypeStruct((m, n), out_dtype),
        compiler_params=_cparams("parallel", "arbitrary"),
        name="proj",
    )(xb, w)


def _fox_proj_kernel(x_ref, w_ref, wf_ref, bf_ref, *rest, nq, heads):
    q_ref, kb_ref, vb_ref, k_ref, v_ref, lf_ref = rest[-6:]
    c = pl.program_id(0)
    tm, tn = q_ref.shape

    def cache_rows(o_ref, y):
        o_ref[...] = y.reshape(tm, tn // LANES, LANES)

    @pl.when(c < nq)
    def _():
        q_ref[...] = _dot(x_ref[...], w_ref[...]).astype(q_ref.dtype)

    @pl.when((c >= nq) & (c < 2 * nq))
    def _():
        y = _dot(x_ref[...], w_ref[...])
        kb_ref[...] = y.astype(kb_ref.dtype)
        cache_rows(k_ref, y)

    @pl.when((c >= 2 * nq) & (c < 3 * nq))
    def _():
        y = _dot(x_ref[...], w_ref[...])
        vb_ref[...] = y.astype(vb_ref.dtype)
        cache_rows(v_ref, y)

    @pl.when(c == 3 * nq)
    def _():
        lf_ref[...] = _log_sigmoid(_dot(x_ref[...], wf_ref[...]) + bf_ref[...])


def _fox_proj(xb, w_qkv, wf, bf, layer, n_layers, heads, kv_stacks):
    m, d = xb.shape
    tm = _pick_tile(m, (512, 256, 128, 64, 32, 16))
    tn = 1024 if d % 1024 == 0 else d
    nq = d // tn
    hpt = tn // LANES
    assert hpt % 8 == 0 or hpt == heads
    nt = m // tm

    def rows(c, i, lo):
        return jnp.where(c < lo, 0, jnp.where(c < lo + nq, i, nt - 1))

    def cols(c, lo):
        return jnp.clip(c - lo, 0, nq - 1)

    in_specs = [
        pl.BlockSpec((tm, d), lambda c, i: (i, 0)),
        pl.BlockSpec((None, d, tn), lambda c, i: (layer, 0, jnp.minimum(c, 3 * nq - 1))),
        pl.BlockSpec((None, d, LANES), lambda c, i: (layer, 0, 0)),
        pl.BlockSpec((None, 1, LANES), lambda c, i: (layer, 0, 0)),
    ]
    args = [xb, w_qkv, wf, bf]
    aliases = {}
    if kv_stacks is not None:
        in_specs += [pl.BlockSpec(memory_space=pl.ANY), pl.BlockSpec(memory_space=pl.ANY)]
        args += list(kv_stacks)
        aliases = {4: 3, 5: 4}
    return pl.pallas_call(
        functools.partial(_fox_proj_kernel, nq=nq, heads=heads),
        grid=(3 * nq + 1, nt),
        in_specs=in_specs,
        out_specs=[
            pl.BlockSpec((tm, tn), lambda c, i: (rows(c, i, 0), cols(c, 0))),
            pl.BlockSpec((tm, tn), lambda c, i: (rows(c, i, nq), cols(c, nq))),
            pl.BlockSpec((tm, tn), lambda c, i: (rows(c, i, 2 * nq), cols(c, 2 * nq))),
            pl.BlockSpec((None, tm, hpt, LANES), lambda c, i: (layer, rows(c, i, nq), cols(c, nq), 0)),
            pl.BlockSpec((None, tm, hpt, LANES), lambda c, i: (layer, rows(c, i, 2 * nq), cols(c, 2 * nq), 0)),
            pl.BlockSpec((tm, LANES), lambda c, i: (jnp.where(c == 3 * nq, i, 0), 0)),
        ],
        out_shape=[
            jax.ShapeDtypeStruct((m, d), BF16),
            jax.ShapeDtypeStruct((m, d), BF16),
            jax.ShapeDtypeStruct((m, d), BF16),
            jax.ShapeDtypeStruct((n_layers, m, heads, LANES), F32),
            jax.ShapeDtypeStruct((n_layers, m, heads, LANES), F32),
            jax.ShapeDtypeStruct((m, LANES), F32),
        ],
        input_output_aliases=aliases,
        compiler_params=_cparams("arbitrary", "arbitrary"),
        name="fox_proj",
    )(*args)


def _cumsum_lanes_kernel(x_ref, o_ref):
    x = x_ref[...]
    n = x.shape[-1]
    lane = lax.broadcasted_iota(jnp.int32, x.shape, 1)
    s = 1
    while s < n:
        x = x + jnp.where(lane >= s, pltpu.roll(x, s, 1), 0.0)
        s *= 2
    o_ref[...] = x


def _cumsum_lanes(x):
    r, n = x.shape
    rb = _pick_tile(r, (64, 32, 16, 8))
    return pl.pallas_call(
        _cumsum_lanes_kernel,
        grid=(r // rb,),
        in_specs=[pl.BlockSpec((rb, n), lambda i: (i, 0))],
        out_specs=pl.BlockSpec((rb, n), lambda i: (i, 0)),
        out_shape=jax.ShapeDtypeStruct((r, n), F32),
        compiler_params=_cparams("parallel"),
        name="cumsum_lanes",
    )(x)


def _head_column(c, h):
    lane = lax.broadcasted_iota(jnp.int32, c.shape, 1)
    return jnp.sum(jnp.where(lane == h, c, 0.0), axis=1, keepdims=True)


_LOG2E = math.log2(math.e)


def _fox_attn_kernel(q_ref, kb_ref, vb_ref, crow_ref, ccol_ref, o_ref, vext_ref, *, tq, scale, hp):
    g = pl.program_id(1)
    a_coef = scale * _LOG2E
    inv_scale = 1.0 / scale
    seq = q_ref.shape[0]
    for hh in range(hp):
        vext_ref[:, 2 * hh * LANES:(2 * hh + 1) * LANES] = vb_ref[:, hh * LANES:(hh + 1) * LANES]
        vext_ref[:, (2 * hh + 1) * LANES:2 * (hh + 1) * LANES] = jnp.ones((seq, LANES), BF16)
    cc = ccol_ref[...]
    row = lax.broadcasted_iota(jnp.int32, (tq, tq), 0)
    col = lax.broadcasted_iota(jnp.int32, (tq, tq), 1)
    ci_all = [_head_column(cc, g * hp + hh) * inv_scale for hh in range(hp)]
    cj_all = [crow_ref[hh] * inv_scale for hh in range(hp)]
    starts = list(range(0, seq, tq))
    pairs = [starts[i:i + 1] + starts[len(starts) - 1 - i:len(starts) - i]
             for i in range(len(starts) // 2)] or [starts]
    for pair in pairs:
        items = [(hh, r0) for r0 in pair for hh in range(hp)]
        sls = {hh: slice(hh * LANES, (hh + 1) * LANES) for hh in range(hp)}
        u = {}
        for (hh, r0) in items:
            w = r0 + tq
            u[hh, r0] = _dot_nt(q_ref[r0:w, sls[hh]], kb_ref[0:w, sls[hh]]) - cj_all[hh][:, 0:w]
        r = {}
        for (hh, r0) in items:
            w = r0 + tq
            diag = jnp.where(col <= row, u[hh, r0][:, r0:w], NEG_INF)
            u[hh, r0] = diag if r0 == 0 else jnp.concatenate([u[hh, r0][:, 0:r0], diag], axis=1)
            ci = ci_all[hh][r0:w, :]
            m = jnp.max(u[hh, r0], axis=1, keepdims=True) + ci
            r[hh, r0] = m - ci
        p = {t: jnp.exp2(a_coef * (u[t] - r[t])).astype(BF16) for t in items}
        for (hh, r0) in items:
            w = r0 + tq
            ov = _dot(p[hh, r0], vext_ref[0:w, 2 * hh * LANES:2 * (hh + 1) * LANES])
            o_ref[r0:w, sls[hh]] = (ov[:, :LANES] / ov[:, LANES:]).astype(o_ref.dtype)


def _fox_attn(q, kb, vb, c_row, c_col, batch, seq, heads):
    m, d = q.shape
    hp = 2 if heads % 2 == 0 else 1
    tq = _pick_tile(seq, (256, 128))
    ng = heads // hp
    w = hp * LANES
    c_row = c_row.reshape(batch * ng, hp, 1, seq)
    return pl.pallas_call(
        functools.partial(_fox_attn_kernel, tq=tq, scale=LANES ** -0.5, hp=hp),
        grid=(batch, ng),
        in_specs=[
            pl.BlockSpec((seq, w), lambda b, g: (b, g)),
            pl.BlockSpec((seq, w), lambda b, g: (b, g)),
            pl.BlockSpec((seq, w), lambda b, g: (b, g)),
            pl.BlockSpec((None, hp, 1, seq), lambda b, g: (b * ng + g, 0, 0, 0)),
            pl.BlockSpec((seq, heads), lambda b, g: (b, 0)),
        ],
        out_specs=pl.BlockSpec((seq, w), lambda b, g: (b, g)),
        out_shape=jax.ShapeDtypeStruct((m, d), BF16),
        scratch_shapes=[pltpu.VMEM((seq, 2 * w), BF16)],
        compiler_params=_cparams("parallel", "parallel"),
        name="fox_attn",
    )(q, kb, vb, c_row, c_col)


def _fox_attn_sample_kernel(q_ref, kn_ref, vn_ref, kp_ref, vp_ref, cpast_ref, cnew_ref, ccol_ref,
                            o_ref, *, scale, heads):
    a_coef = scale * _LOG2E
    inv_scale = 1.0 / scale
    ts = q_ref.shape[0]
    past = kp_ref.shape[0] // heads
    cc = ccol_ref[...]
    row = lax.broadcasted_iota(jnp.int32, (ts, ts), 0)
    col = lax.broadcasted_iota(jnp.int32, (ts, ts), 1)
    group = _pick_tile(heads, (4, 2, 1))
    kw = kp_ref[...].reshape(past, heads * LANES).astype(BF16)
    vw = vp_ref[...].reshape(past, heads * LANES).astype(BF16)
    for h0 in range(0, heads, group):
        hs = list(range(h0, h0 + group))
        sls = {h: slice(h * LANES, (h + 1) * LANES) for h in hs}
        kp = {h: kw[:, sls[h]] for h in hs}
        vp = {h: vw[:, sls[h]] for h in hs}
        up = {h: _dot_nt(q_ref[:, sls[h]], kp[h]) - cpast_ref[h] * inv_scale for h in hs}
        un = {h: jnp.where(col <= row, _dot_nt(q_ref[:, sls[h]], kn_ref[:, sls[h]]) - cnew_ref[h] * inv_scale,
                           NEG_INF) for h in hs}
        r = {}
        for h in hs:
            ci = _head_column(cc, h) * inv_scale
            m = jnp.maximum(jnp.max(up[h], axis=1, keepdims=True), jnp.max(un[h], axis=1, keepdims=True)) + ci
            r[h] = m - ci
        for h in hs:
            pp = jnp.exp2(a_coef * (up[h] - r[h]))
            pn = jnp.exp2(a_coef * (un[h] - r[h]))
            l = jnp.sum(pp, axis=1, keepdims=True) + jnp.sum(pn, axis=1, keepdims=True)
            o = (_dot(pp.astype(BF16), vp[h]) + _dot(pn.astype(BF16), vn_ref[:, sls[h]])) / l
            o_ref[:, sls[h]] = o.astype(o_ref.dtype)


def _fox_attn_sample(q, kb, vb, k_past, v_past, layer, c_past, c_new, c_col, batch, ts, past, heads):
    m, d = q.shape
    c_past = c_past.reshape(batch, heads, 1, past)
    c_new = c_new.reshape(batch, heads, 1, ts)
    return pl.pallas_call(
        functools.partial(_fox_attn_sample_kernel, scale=LANES ** -0.5, heads=heads),
        grid=(batch,),
        in_specs=[
            pl.BlockSpec((ts, d), lambda b: (b, 0)),
            pl.BlockSpec((ts, d), lambda b: (b, 0)),
            pl.BlockSpec((ts, d), lambda b: (b, 0)),
            pl.BlockSpec((None, past * heads, LANES), lambda b: (layer, b, 0)),
            pl.BlockSpec((None, past * heads, LANES), lambda b: (layer, b, 0)),
            pl.BlockSpec((None, heads, 1, past), lambda b: (b, 0, 0, 0)),
            pl.BlockSpec((None, heads, 1, ts), lambda b: (b, 0, 0, 0)),
            pl.BlockSpec((ts, heads), lambda b: (b, 0)),
        ],
        out_specs=pl.BlockSpec((ts, d), lambda b: (b, 0)),
        out_shape=jax.ShapeDtypeStruct((m, d), BF16),
        compiler_params=_cparams("parallel"),
        name="fox_attn_sample",
    )(q, kb, vb, k_past, v_past, c_past, c_new, c_col)


def _expand_rows(f, reps):
    parts = [jnp.broadcast_to(f[r:r + 1, :], (reps, f.shape[1])) for r in range(f.shape[0])]
    return parts[0] if len(parts) == 1 else jnp.concatenate(parts, axis=0)


def _take_rows(f, idx):
    parts = [f[r:r + 1, :] for r in idx]
    return parts[0] if len(parts) == 1 else jnp.concatenate(parts, axis=0)


def _hgrn_group(q, z, v, go, states, bscr_ref, lb, ng, kscale, n):
    n_g = q.shape[0] // n
    hp = q.shape[1] // LANES
    nblk = n // HGRN_SUB
    tiles = [(u, hh) for u in range(n_g) for hh in range(hp)]

    def tile(x, u, hh):
        return x[u * n:(u + 1) * n, hh * LANES:(hh + 1) * LANES]

    e = jnp.exp(-jnp.abs(z))
    r1 = 1.0 / (1.0 + e)
    er1 = e * r1
    pos = z > 0.0
    g = jnp.log(lb + (1.0 - lb) * jnp.where(pos, r1, er1))
    kk = (1.0 - lb) * jnp.where(pos, er1, r1)
    qf = q * _sigmoid(q) * kscale
    gate = go * _sigmoid(go)
    vb = v.astype(BF16)

    row = lax.broadcasted_iota(jnp.int32, (n, n), 0)
    col = lax.broadcasted_iota(jnp.int32, (n, n), 1)
    tri = jnp.where(col <= row, 1.0, 0.0).astype(BF16)
    g_hi = g.astype(BF16)
    g_r1 = g - g_hi.astype(F32)
    g_mid = g_r1.astype(BF16)
    g_lo = (g_r1 - g_mid.astype(F32)).astype(BF16)
    b = {t: _dot(tri, tile(g_hi, *t)) + _dot(tri, tile(g_mid, *t)) + _dot(tri, tile(g_lo, *t))
         for t in tiles}
    for (u, hh) in tiles:
        bscr_ref[hh, u] = b[u, hh]
    bmid = {(u, hh): bscr_ref[hh, u, pl.ds(HGRN_SUB // 2, nblk, stride=HGRN_SUB), :] for (u, hh) in tiles}
    bend = {(u, hh): bscr_ref[hh, u, pl.ds(HGRN_SUB - 1, nblk, stride=HGRN_SUB), :] for (u, hh) in tiles}
    blast = {t: bend[t][nblk - 1:nblk, :] for t in tiles}
    qd, kd = {}, {}
    for t in tiles:
        bmid_rows = _expand_rows(bmid[t], HGRN_SUB)
        qd[t] = tile(qf, *t) * jnp.exp(b[t] - bmid_rows)
        kd[t] = tile(kk, *t) * jnp.exp(bmid_rows - b[t])

    blk = lax.broadcasted_iota(jnp.int32, (nblk, LANES), 0)
    a = {t: None for t in tiles}
    s = nblk // 2
    while s >= 1:
        upper = ((blk >> (s.bit_length() - 1)) & 1) == 1
        group_shift = (2 * s * HGRN_SUB).bit_length() - 1
        same_group = (row >> group_shift) == (col >> group_shift)
        for t in tiles:
            ref = _take_rows(bend[t], [(i // (2 * s)) * 2 * s + s - 1 for i in range(nblk)])
            fq = jnp.where(upper, jnp.exp(jnp.minimum(bmid[t] - ref, 0.0)), 0.0)
            fk = jnp.where(upper, 0.0, jnp.exp(jnp.minimum(ref - bmid[t], 0.0)))
            ql = qd[t] * _expand_rows(fq, HGRN_SUB)
            kl = kd[t] * _expand_rows(fk, HGRN_SUB)
            prod = _dot_nt(ql.astype(BF16), kl.astype(BF16))
            a[t] = prod if a[t] is None else jnp.where(same_group, prod, a[t])
        s //= 2
    sub_shift = HGRN_SUB.bit_length() - 1
    on_diag = ((row >> sub_shift) == (col >> sub_shift)) & (col <= row)
    o_in, q_in, upd, decay = {}, {}, {}, {}
    for t in tiles:
        pd = _dot_nt(qd[t].astype(BF16), kd[t].astype(BF16))
        at = jnp.where(on_diag, pd, 0.0 if a[t] is None else a[t])
        vt = tile(vb, *t)
        o_in[t] = _dot(at.astype(BF16), vt)
        q_in[t] = (qd[t] * _expand_rows(jnp.exp(bmid[t]), HGRN_SUB)).astype(BF16)
        k_out = kd[t] * _expand_rows(jnp.exp(blast[t] - bmid[t]), HGRN_SUB)
        upd[t] = _dot_tn(vt, k_out.astype(BF16))
        decay[t] = jnp.exp(blast[t])

    states = list(states)
    rows_out = []
    for u in range(n_g):
        cols_out = []
        for hh in range(hp):
            t = (u, hh)
            o = o_in[t] + _dot_nt(q_in[t], states[hh].astype(BF16))
            states[hh] = states[hh] * decay[t] + upd[t]
            o = o * lax.rsqrt(jnp.mean(o * o, axis=-1, keepdims=True) + RMS_EPS) * ng
            cols_out.append(o * tile(gate, *t))
        rows_out.append(cols_out[0] if hp == 1 else jnp.concatenate(cols_out, axis=1))
    out = rows_out[0] if n_g == 1 else jnp.concatenate(rows_out, axis=0)
    return out, states


def _hgrn_kernel(*refs, chunk, n_chunks, has_s0, layer, kscale, hp):
    if has_s0:
        q_ref, z_ref, i_ref, g_ref, lbp_ref, ng_ref, s0_ref, o_ref, s_ref, bscr_ref = refs
    else:
        q_ref, z_ref, i_ref, g_ref, lbp_ref, ng_ref, o_ref, s_ref, bscr_ref = refs
        s0_ref = None

    lbp = lbp_ref[...]
    e = jnp.exp(lbp - jnp.max(lbp, axis=0, keepdims=True))
    soft = e / jnp.sum(e, axis=0, keepdims=True)
    r = lax.broadcasted_iota(jnp.int32, lbp.shape, 0)
    lb_all = jnp.sum(jnp.where((r >= 1) & (r <= layer), soft, 0.0), axis=0, keepdims=True)
    ng = ng_ref[...]

    group = bscr_ref.shape[1]

    def step(t, states):
        rows = pl.ds(pl.multiple_of(t * group * chunk, group * chunk), group * chunk)
        o, states = _hgrn_group(q_ref[rows, :], z_ref[rows, :], i_ref[rows, :], g_ref[rows, :],
                                states, bscr_ref, lb_all, ng, kscale, chunk)
        o_ref[rows, :] = o.astype(o_ref.dtype)
        return tuple(states)

    if has_s0:
        init = tuple(s0_ref[hh].T for hh in range(hp))
    else:
        init = tuple(jnp.zeros((LANES, LANES), F32) for _ in range(hp))
    if n_chunks == group:
        states = step(0, init)
    else:
        states = lax.fori_loop(0, n_chunks // group, step, init)
    for hh in range(hp):
        s_ref[hh] = states[hh].T


def _hgrn(proj, lb_param, norm_g, s0, batch, seq, heads, layer, mixer_idx):
    m = proj.shape[0]
    d = heads * LANES
    chunk = _pick_tile(seq, (HGRN_CHUNK, 64, 32, 16))
    assert chunk % HGRN_SUB == 0 and seq % chunk == 0
    depth = lb_param.shape[0]
    has_s0 = s0 is not None
    hp = _pick_tile(heads, (2, 1) if seq > chunk else (4, 2, 1))
    group = _pick_tile(seq // chunk, (4, 2, 1))
    ng = heads // hp
    w = hp * LANES
    in_specs = [
        pl.BlockSpec((seq, w), lambda b, g: (b, g)),
        pl.BlockSpec((seq, w), lambda b, g: (b, ng + g)),
        pl.BlockSpec((seq, w), lambda b, g: (b, 2 * ng + g)),
        pl.BlockSpec((seq, w), lambda b, g: (b, 3 * ng + g)),
        pl.BlockSpec((depth, w), lambda b, g: (0, g)),
        pl.BlockSpec((None, 1, LANES), lambda b, g: (mixer_idx, 0, 0)),
    ]
    args = [proj, proj, proj, proj, lb_param, norm_g]
    if has_s0:
        in_specs.append(pl.BlockSpec((None, hp, LANES, LANES), lambda b, g: (b, g, 0, 0)))
        args.append(s0)
    return pl.pallas_call(
        functools.partial(_hgrn_kernel, chunk=chunk, n_chunks=seq // chunk, has_s0=has_s0,
                          layer=layer, kscale=LANES ** -0.5, hp=hp),
        grid=(batch, ng),
        in_specs=in_specs,
        out_specs=[
            pl.BlockSpec((seq, w), lambda b, g: (b, g)),
            pl.BlockSpec((None, hp, LANES, LANES), lambda b, g: (b, g, 0, 0)),
        ],
        out_shape=[
            jax.ShapeDtypeStruct((m, d), BF16),
            jax.ShapeDtypeStruct((batch, heads, LANES, LANES), F32),
        ],
        scratch_shapes=[pltpu.VMEM((hp, group, chunk, LANES), F32)],
        compiler_params=_cparams("parallel", "parallel"),
        name="hgrn",
    )(*args)


def _outproj_ln_kernel(o_ref, w_ref, x_ref, g_ref, b_ref, y_ref, *, alpha, rows):
    for r0 in range(0, o_ref.shape[0], rows):
        sl = slice(r0, r0 + rows)
        r = alpha * x_ref[sl, :] + _dot(o_ref[sl, :], w_ref[...])
        y_ref[sl, :] = _layer_norm(r, g_ref[...], b_ref[...])


def _outproj_ln(o, w, x, ln_g, ln_b, layer, ln_idx, alpha):
    m, d = x.shape
    tm = _pick_tile(m, (512, 256, 128, 64, 32, 16))
    return pl.pallas_call(
        functools.partial(_outproj_ln_kernel, alpha=alpha, rows=_pick_tile(tm, (128,))),
        grid=(m // tm,),
        in_specs=[
            pl.BlockSpec((tm, d), lambda i: (i, 0)),
            pl.BlockSpec((None, d, d), lambda i: (layer, 0, 0)),
            pl.BlockSpec((tm, d), lambda i: (i, 0)),
            pl.BlockSpec((None, 1, d), lambda i: (ln_idx, 0, 0)),
            pl.BlockSpec((None, 1, d), lambda i: (ln_idx, 0, 0)),
        ],
        out_specs=pl.BlockSpec((tm, d), lambda i: (i, 0)),
        out_shape=jax.ShapeDtypeStruct((m, d), F32),
        compiler_params=_cparams("parallel"),
        name="outproj_ln",
    )(o, w, x, ln_g, ln_b)


def _cast_pad_cols_kernel(x_ref, o_ref):
    f = x_ref.shape[1]
    o_ref[:, :f] = x_ref[...].astype(o_ref.dtype)
    if o_ref.shape[1] > f:
        o_ref[:, f:] = jnp.zeros((o_ref.shape[0], o_ref.shape[1] - f), o_ref.dtype)


def _cast_pad_cols(w, half, f, fp):
    n, d, _ = w.shape
    rb = _pick_tile(d, (256, 128, 64, 32, 16))
    return pl.pallas_call(
        _cast_pad_cols_kernel,
        grid=(n, d // rb),
        in_specs=[pl.BlockSpec((None, rb, f), lambda l, r: (l, r, half))],
        out_specs=pl.BlockSpec((None, rb, fp), lambda l, r: (l, r, 0)),
        out_shape=jax.ShapeDtypeStruct((n, d, fp), BF16),
        compiler_params=_cparams("parallel", "parallel"),
        name="cast_pad_cols",
    )(w)


def _cast_rows_kernel(x_ref, *rest, row0, valid):
    o_ref = rest[-1]
    rb = x_ref.shape[0]
    row = row0 + pl.program_id(1) * rb + lax.broadcasted_iota(jnp.int32, x_ref.shape, 0)
    o_ref[...] = jnp.where(row < valid, x_ref[...], 0.0).astype(o_ref.dtype)


def _cast_pad_rows(w, fp, tf):
    n, f, d = w.shape
    f_main = f // tf * tf
    rb = _pick_tile(f_main, (1280, 1024, 640, 512, 256, 128, 64, 32, 16))
    out = pl.pallas_call(
        functools.partial(_cast_rows_kernel, row0=0, valid=f),
        grid=(n, f_main // rb),
        in_specs=[pl.BlockSpec((None, rb, d), lambda l, r: (l, r, 0))],
        out_specs=pl.BlockSpec((None, rb, d), lambda l, r: (l, r, 0)),
        out_shape=jax.ShapeDtypeStruct((n, fp, d), BF16),
        compiler_params=_cparams("parallel", "parallel"),
        name="cast_rows",
    )(w)
    if fp == f_main:
        return out
    rb = math.gcd(math.gcd(f_main, f), fp)
    assert rb % 16 == 0
    first, last = f_main // rb, f // rb - 1
    return pl.pallas_call(
        functools.partial(_cast_rows_kernel, row0=f_main, valid=f),
        grid=(n, (fp - f_main) // rb),
        in_specs=[pl.BlockSpec((None, rb, d), lambda l, r: (l, jnp.minimum(first + r, last), 0)),
                  pl.BlockSpec(memory_space=pl.ANY)],
        out_specs=pl.BlockSpec((None, rb, d), lambda l, r: (l, first + r, 0)),
        out_shape=jax.ShapeDtypeStruct((n, fp, d), BF16),
        input_output_aliases={1: 0},
        compiler_params=_cparams("parallel", "parallel"),
        name="cast_rows_tail",
    )(w, out)


def _prep_ffn(w_up, w_down, tf):
    f = w_down.shape[1]
    fp = _round_up(f, tf)
    assert f % LANES == 0 and f >= tf
    return _cast_pad_cols(w_up, 0, f, fp), _cast_pad_cols(w_up, 1, f, fp), _cast_pad_rows(w_down, fp, tf)


def _cumsum_heads(logf_bsh):
    b, s, h = logf_bsh.shape
    sp = _round_up(s, LANES)
    x = jnp.transpose(logf_bsh, (0, 2, 1)).reshape(b * h, s)
    x = jnp.pad(x, ((0, 0), (0, sp - s)))
    return _cumsum_lanes(x)[:, :s]


def kernel(x_prompt, x_sample, cache_fox_k, cache_fox_v, cache_fox_logf, state_hgrn, ln_g, ln_b, ffn1_up, ffn1_down, ffn2_up, ffn2_down, fox_w_in, fox_b_f, fox_w_out, hgrn_w_in, hgrn_lb, hgrn_norm_g, hgrn_w_out):
    bp, tp, d = x_prompt.shape
    bs, ts, _ = x_sample.shape
    depth = ln_g.shape[0]
    heads = fox_b_f.shape[1]
    n_fox = fox_w_in.shape[0]
    past = cache_fox_k.shape[2]
    assert d == heads * LANES and hgrn_norm_g.shape[1] == LANES
    alpha = (2 * depth) ** 0.25

    tf = 512 if ffn1_down.shape[1] >= 512 else LANES
    ffn1 = _prep_ffn(ffn1_up, ffn1_down, tf)
    ffn2 = _prep_ffn(ffn2_up, ffn2_down, tf)
    fox_qkv = fox_w_in[..., :3 * d].astype(BF16)
    fox_wf = jnp.pad(fox_w_in[..., 3 * d:], ((0, 0), (0, 0), (0, LANES - heads))).astype(BF16)
    fox_bf = jnp.pad(fox_b_f, ((0, 0), (0, LANES - heads)))[:, None, :]
    fox_out = fox_w_out.astype(BF16)
    hgrn_in = hgrn_w_in.astype(BF16)
    hgrn_out = hgrn_w_out.astype(BF16)
    hgrn_ng = hgrn_norm_g[:, None, :]
    g3 = ln_g.reshape(depth * 3, 1, d)
    b3 = ln_b.reshape(depth * 3, 1, d)

    xp = x_prompt.reshape(bp * tp, d)
    xs = x_sample.reshape(bs * ts, d)
    kv_p, kv_s = None, None
    logf_p, logf_s, hg_p, hg_s = [], [], [], []

    for i in range(depth):
        j = i // 2
        xp, xpb = _ffn_ln(xp, ffn1, g3, b3, i, 3 * i, alpha, tf, with_bf16=True)
        xs, xsb = _ffn_ln(xs, ffn1, g3, b3, i, 3 * i, alpha, tf, with_bf16=True)
        if i % 2 == 0:
            q, kb, vb, k_st, v_st, lf = _fox_proj(xpb, fox_qkv, fox_wf, fox_bf, j, n_fox, heads, kv_p)
            kv_p = (k_st, v_st)
            logf = lf[:, :heads].reshape(bp, tp, heads)
            c_row = _cumsum_heads(logf)
            c_col = jnp.transpose(c_row.reshape(bp, heads, tp), (0, 2, 1)).reshape(bp * tp, heads)
            op = _fox_attn(q, kb, vb, c_row, c_col, bp, tp, heads)
            logf_p.append(logf)
            q, kb, vb, k_st, v_st, lf = _fox_proj(xsb, fox_qkv, fox_wf, fox_bf, j, n_fox, heads, kv_s)
            kv_s = (k_st, v_st)
            logf = lf[:, :heads].reshape(bs, ts, heads)
            c_all = _cumsum_heads(jnp.concatenate([cache_fox_logf[j].astype(F32), logf], axis=1))
            c_new = c_all[:, past:]
            c_col = jnp.transpose(c_new.reshape(bs, heads, ts), (0, 2, 1)).reshape(bs * ts, heads)
            os_ = _fox_attn_sample(q, kb, vb, cache_fox_k.reshape(n_fox, bs * past * heads, LANES),
                                   cache_fox_v.reshape(n_fox, bs * past * heads, LANES), j,
                                   c_all[:, :past], c_new, c_col, bs, ts, past, heads)
            logf_s.append(logf)
            w_out = fox_out
        else:
            pp = _proj(xpb, hgrn_in, j, 4 * d)
            op, sp = _hgrn(pp, hgrn_lb, hgrn_ng, None, bp, tp, heads, i, j)
            ps = _proj(xsb, hgrn_in, j, 4 * d)
            os_, ss = _hgrn(ps, hgrn_lb, hgrn_ng, state_hgrn[j], bs, ts, heads, i, j)
            hg_p.append(sp)
            hg_s.append(ss)
            w_out = hgrn_out
        xp = _outproj_ln(op, w_out, xp, g3, b3, j, 3 * i + 1, alpha)
        xs = _outproj_ln(os_, w_out, xs, g3, b3, j, 3 * i + 1, alpha)
        xp = _ffn_ln(xp, ffn2, g3, b3, i, 3 * i + 2, alpha, tf)
        xs = _ffn_ln(xs, ffn2, g3, b3, i, 3 * i + 2, alpha, tf)

    return (xp.reshape(bp, tp, d), xs.reshape(bs, ts, d),
            kv_p[0].reshape(n_fox, bp, tp, heads, LANES), kv_p[1].reshape(n_fox, bp, tp, heads, LANES),
            jnp.stack(logf_p), jnp.stack(hg_p),
            kv_s[0].reshape(n_fox, bs, ts, heads, LANES), kv_s[1].reshape(n_fox, bs, ts, heads, LANES),
            jnp.stack(logf_s), jnp.stack(hg_s))
```

```python
import functools
import math

import jax
import jax.numpy as jnp
from jax import lax
from jax.experimental import pallas as pl
from jax.experimental.pallas import tpu as pltpu

F32 = jnp.float32
BF16 = jnp.bfloat16

LN_EPS = 1e-5
RMS_EPS = 1e-6
NEG_INF = -1e30

LANES = 128
HGRN_SUB = 16
HGRN_CHUNK = 128
VMEM_LIMIT_BYTES = 48 * 1024 * 1024

def _round_up(n, m):
    return (n + m - 1) // m * m


def _pick_tile(n, candidates):
    for c in candidates:
        if n % c == 0:
            return c
    return n


def _cparams(*semantics):
    return pltpu.CompilerParams(dimension_semantics=semantics,
                                vmem_limit_bytes=VMEM_LIMIT_BYTES)


def _dot(a, b):
    return jnp.dot(a, b, preferred_element_type=F32)


def _dot_nt(a, b):
    return lax.dot_general(a, b, (((1,), (1,)), ((), ())), preferred_element_type=F32)


def _dot_tn(a, b):
    return lax.dot_general(a, b, (((0,), (0,)), ((), ())), preferred_element_type=F32)


def _sigmoid(x):
    return 1.0 / (1.0 + jnp.exp(-x))


def _log_sigmoid(x):
    return jnp.minimum(x, 0.0) - jnp.log(1.0 + jnp.exp(-jnp.abs(x)))


def _layer_norm(r, g, b):
    mu = jnp.mean(r, axis=-1, keepdims=True)
    c = r - mu
    var = jnp.mean(c * c, axis=-1, keepdims=True)
    return c * lax.rsqrt(var + LN_EPS) * g + b


def _ffn_ln_kernel(x_ref, wg_ref, wu_ref, wd_ref, g_ref, b_ref, *rest, alpha, n_tiles, ln_rows):
    out_refs, (xb_ref, acc_ref) = rest[:-2], rest[-2:]
    i = pl.program_id(0)
    f = pl.program_id(1)
    slot = lax.rem(i, 2)
    tm = xb_ref.shape[0]

    def norm_rows(src_slot):
        start = pl.multiple_of(jnp.minimum(f * ln_rows, tm - ln_rows), 16)
        rows = pl.ds(start, ln_rows)
        y = _layer_norm(0.5 * acc_ref[src_slot, rows, :], g_ref[...], b_ref[...])
        for o_ref in out_refs:
            o_ref[rows, :] = y.astype(o_ref.dtype)

    @pl.when((i == 0) & (f == 0))
    def _():
        acc_ref[1] = jnp.zeros(acc_ref.shape[1:], F32)

    @pl.when(i < n_tiles)
    def _():
        @pl.when(f == 0)
        def _():
            x = x_ref[...]
            xb_ref[...] = x.astype(BF16)
            acc_ref[slot] = (2.0 * alpha) * x

        norm_rows(1 - slot)
        xb = xb_ref[...]
        gate = _dot(xb, wg_ref[...])
        up = _dot(xb, wu_ref[...])
        h = gate * _sigmoid(gate) * up
        acc_ref[slot] += _dot(h.astype(BF16), wd_ref[...])

    @pl.when(i == n_tiles)
    def _():
        norm_rows(1 - slot)


def _ffn_ln(x, weights, ln_g, ln_b, layer, ln_idx, alpha, tf, with_bf16=False):
    w_gate, w_up, w_down = weights
    m, d = x.shape
    fp = w_down.shape[1]
    nf = fp // tf
    tm = _pick_tile(m, (512, 256, 128, 64, 32, 16))
    nt = m // tm
    ln_rows = min(tm, _round_up(-(-tm // nf), 16))
    out_dtypes = (F32, BF16) if with_bf16 else (F32,)

    def w_step(i, f):
        return jnp.where(i < nt, f, nf - 1)

    outs = pl.pallas_call(
        functools.partial(_ffn_ln_kernel, alpha=alpha, n_tiles=nt, ln_rows=ln_rows),
        grid=(nt + 1, nf),
        in_specs=[
            pl.BlockSpec((tm, d), lambda i, f: (jnp.minimum(i, nt - 1), 0)),
            pl.BlockSpec((None, d, tf), lambda i, f: (layer, 0, w_step(i, f))),
            pl.BlockSpec((None, d, tf), lambda i, f: (layer, 0, w_step(i, f))),
            pl.BlockSpec((None, tf, d), lambda i, f: (layer, w_step(i, f), 0)),
            pl.BlockSpec((None, 1, d), lambda i, f: (ln_idx, 0, 0)),
            pl.BlockSpec((None, 1, d), lambda i, f: (ln_idx, 0, 0)),
        ],
        out_specs=[pl.BlockSpec((tm, d), lambda i, f: (jnp.maximum(i - 1, 0), 0)) for _ in out_dtypes],
        out_shape=[jax.ShapeDtypeStruct((m, d), dt) for dt in out_dtypes],
        scratch_shapes=[pltpu.VMEM((tm, d), BF16), pltpu.VMEM((2, tm, d), F32)],
        compiler_params=_cparams("arbitrary", "arbitrary"),
        name="ffn_ln",
    )(x, w_gate, w_up, w_down, ln_g, ln_b)
    return outs if with_bf16 else outs[0]


def _proj_kernel(x_ref, w_ref, o_ref):
    o_ref[...] = _dot(x_ref[...], w_ref[...].astype(BF16)).astype(o_ref.dtype)


def _proj(xb, w, layer, n, out_dtype=F32):
    m, d = xb.shape
    tm = _pick_tile(m, (2048, 1024, 512, 256, 128, 64, 32, 16))
    tn = _pick_tile(n, (512, 256, 128))
    return pl.pallas_call(
        _proj_kernel,
        grid=(m // tm, n // tn),
        in_specs=[
            pl.BlockSpec((tm, d), lambda i, j: (i, 0)),
            pl.BlockSpec((None, d, tn), lambda i, j: (layer, 0, j)),
        ],
        out_specs=pl.BlockSpec((tm, tn), lambda i, j: (i, j)),
        out_shape=jax.ShapeDtypeStruct((m, n), out_dtype),
        compiler_params=_cparams("parallel", "arbitrary"),
        name="proj",
    )(xb, w)


def _fox_proj_kernel(x_ref, w_ref, wf_ref, bf_ref, *rest, nq, heads):
    q_ref, kb_ref, vb_ref, k_ref, v_ref, lf_ref, wb_ref = rest[-7:]
    c = pl.program_id(0)
    tm, tn = q_ref.shape

    @pl.when((pl.program_id(1) == 0) & (c < 3 * nq))
    def _():
        wb_ref[...] = w_ref[...].astype(BF16)

    def cache_rows(o_ref, y):
        o_ref[...] = y.reshape(tm, tn // LANES, LANES)

    @pl.when(c < nq)
    def _():
        q_ref[...] = _dot(x_ref[...], wb_ref[...]).astype(q_ref.dtype)

    @pl.when((c >= nq) & (c < 2 * nq))
    def _():
        y = _dot(x_ref[...], wb_ref[...])
        kb_ref[...] = y.astype(kb_ref.dtype)
        cache_rows(k_ref, y)

    @pl.when((c >= 2 * nq) & (c < 3 * nq))
    def _():
        y = _dot(x_ref[...], wb_ref[...])
        vb_ref[...] = y.astype(vb_ref.dtype)
        cache_rows(v_ref, y)

    @pl.when(c == 3 * nq)
    def _():
        lf_ref[...] = _log_sigmoid(_dot(x_ref[...], wf_ref[...]) + bf_ref[...])


def _fox_proj(xb, w_qkv, wf, bf, layer, n_layers, heads, kv_stacks):
    m, d = xb.shape
    tm = _pick_tile(m, (512, 256, 128, 64, 32, 16))
    tn = 1024 if d % 1024 == 0 else d
    nq = d // tn
    hpt = tn // LANES
    assert hpt % 8 == 0 or hpt == heads
    nt = m // tm

    def rows(c, i, lo):
        return jnp.where(c < lo, 0, jnp.where(c < lo + nq, i, nt - 1))

    def cols(c, lo):
        return jnp.clip(c - lo, 0, nq - 1)

    in_specs = [
        pl.BlockSpec((tm, d), lambda c, i: (i, 0)),
        pl.BlockSpec((None, d, tn), lambda c, i: (layer, 0, jnp.minimum(c, 3 * nq - 1))),
        pl.BlockSpec((None, d, LANES), lambda c, i: (layer, 0, 0)),
        pl.BlockSpec((None, 1, LANES), lambda c, i: (layer, 0, 0)),
    ]
    args = [xb, w_qkv, wf, bf]
    aliases = {}
    if kv_stacks is not None:
        in_specs += [pl.BlockSpec(memory_space=pl.ANY), pl.BlockSpec(memory_space=pl.ANY)]
        args += list(kv_stacks)
        aliases = {4: 3, 5: 4}
    return pl.pallas_call(
        functools.partial(_fox_proj_kernel, nq=nq, heads=heads),
        grid=(3 * nq + 1, nt),
        in_specs=in_specs,
        out_specs=[
            pl.BlockSpec((tm, tn), lambda c, i: (rows(c, i, 0), cols(c, 0))),
            pl.BlockSpec((tm, tn), lambda c, i: (rows(c, i, nq), cols(c, nq))),
            pl.BlockSpec((tm, tn), lambda c, i: (rows(c, i, 2 * nq), cols(c, 2 * nq))),
            pl.BlockSpec((None, tm, hpt, LANES), lambda c, i: (layer, rows(c, i, nq), cols(c, nq), 0)),
            pl.BlockSpec((None, tm, hpt, LANES), lambda c, i: (layer, rows(c, i, 2 * nq), cols(c, 2 * nq), 0)),
            pl.BlockSpec((tm, LANES), lambda c, i: (jnp.where(c == 3 * nq, i, 0), 0)),
        ],
        out_shape=[
            jax.ShapeDtypeStruct((m, d), BF16),
            jax.ShapeDtypeStruct((m, d), BF16),
            jax.ShapeDtypeStruct((m, d), BF16),
            jax.ShapeDtypeStruct((n_layers, m, heads, LANES), F32),
            jax.ShapeDtypeStruct((n_layers, m, heads, LANES), F32),
            jax.ShapeDtypeStruct((m, LANES), F32),
        ],
        input_output_aliases=aliases,
        scratch_shapes=[pltpu.VMEM((d, tn), BF16)],
        compiler_params=_cparams("arbitrary", "arbitrary"),
        name="fox_proj",
    )(*args)


def _cumsum_lanes_kernel(x_ref, o_ref):
    x = x_ref[...]
    n = x.shape[-1]
    lane = lax.broadcasted_iota(jnp.int32, x.shape, 1)
    s = 1
    while s < n:
        x = x + jnp.where(lane >= s, pltpu.roll(x, s, 1), 0.0)
        s *= 2
    o_ref[...] = x


def _cumsum_lanes(x):
    r, n = x.shape
    rb = _pick_tile(r, (64, 32, 16, 8))
    return pl.pallas_call(
        _cumsum_lanes_kernel,
        grid=(r // rb,),
        in_specs=[pl.BlockSpec((rb, n), lambda i: (i, 0))],
        out_specs=pl.BlockSpec((rb, n), lambda i: (i, 0)),
        out_shape=jax.ShapeDtypeStruct((r, n), F32),
        compiler_params=_cparams("parallel"),
        name="cumsum_lanes",
    )(x)


def _head_column(c, h):
    lane = lax.broadcasted_iota(jnp.int32, c.shape, 1)
    return jnp.sum(jnp.where(lane == h, c, 0.0), axis=1, keepdims=True)


_LOG2E = math.log2(math.e)


def _fox_attn_kernel(q_ref, kb_ref, vb_ref, crow_ref, ccol_ref, o_ref, vext_ref, *, tq, scale, hp):
    g = pl.program_id(1)
    a_coef = scale * _LOG2E
    inv_scale = 1.0 / scale
    seq = q_ref.shape[0]
    for hh in range(hp):
        vext_ref[:, 2 * hh * LANES:(2 * hh + 1) * LANES] = vb_ref[:, hh * LANES:(hh + 1) * LANES]
        vext_ref[:, (2 * hh + 1) * LANES:2 * (hh + 1) * LANES] = jnp.ones((seq, LANES), BF16)
    cc = ccol_ref[...]
    row = lax.broadcasted_iota(jnp.int32, (tq, tq), 0)
    col = lax.broadcasted_iota(jnp.int32, (tq, tq), 1)
    ci_all = [_head_column(cc, g * hp + hh) * inv_scale for hh in range(hp)]
    cj_all = [crow_ref[hh] * inv_scale for hh in range(hp)]
    starts = list(range(0, seq, tq))
    pairs = [starts[i:i + 1] + starts[len(starts) - 1 - i:len(starts) - i]
             for i in range(len(starts) // 2)] or [starts]
    for pair in pairs:
        items = [(hh, r0) for r0 in pair for hh in range(hp)]
        sls = {hh: slice(hh * LANES, (hh + 1) * LANES) for hh in range(hp)}
        u = {}
        for (hh, r0) in items:
            w = r0 + tq
            u[hh, r0] = _dot_nt(q_ref[r0:w, sls[hh]], kb_ref[0:w, sls[hh]]) - cj_all[hh][:, 0:w]
        r = {}
        for (hh, r0) in items:
            w = r0 + tq
            diag = jnp.where(col <= row, u[hh, r0][:, r0:w], NEG_INF)
            u[hh, r0] = diag if r0 == 0 else jnp.concatenate([u[hh, r0][:, 0:r0], diag], axis=1)
            ci = ci_all[hh][r0:w, :]
            m = jnp.max(u[hh, r0], axis=1, keepdims=True) + ci
            r[hh, r0] = m - ci
        p = {t: jnp.exp2(a_coef * (u[t] - r[t])).astype(BF16) for t in items}
        for (hh, r0) in items:
            w = r0 + tq
            ov = _dot(p[hh, r0], vext_ref[0:w, 2 * hh * LANES:2 * (hh + 1) * LANES])
            o_ref[r0:w, sls[hh]] = (ov[:, :LANES] / ov[:, LANES:]).astype(o_ref.dtype)


def _fox_attn(q, kb, vb, c_row, c_col, batch, seq, heads):
    m, d = q.shape
    hp = 2 if heads % 2 == 0 else 1
    tq = _pick_tile(seq, (256, 128))
    ng = heads // hp
    w = hp * LANES
    c_row = c_row.reshape(batch * ng, hp, 1, seq)
    return pl.pallas_call(
        functools.partial(_fox_attn_kernel, tq=tq, scale=LANES ** -0.5, hp=hp),
        grid=(batch, ng),
        in_specs=[
            pl.BlockSpec((seq, w), lambda b, g: (b, g)),
            pl.BlockSpec((seq, w), lambda b, g: (b, g)),
            pl.BlockSpec((seq, w), lambda b, g: (b, g)),
            pl.BlockSpec((None, hp, 1, seq), lambda b, g: (b * ng + g, 0, 0, 0)),
            pl.BlockSpec((seq, heads), lambda b, g: (b, 0)),
        ],
        out_specs=pl.BlockSpec((seq, w), lambda b, g: (b, g)),
        out_shape=jax.ShapeDtypeStruct((m, d), BF16),
        scratch_shapes=[pltpu.VMEM((seq, 2 * w), BF16)],
        compiler_params=_cparams("parallel", "parallel"),
        name="fox_attn",
    )(q, kb, vb, c_row, c_col)


def _fox_attn_sample_kernel(q_ref, kn_ref, vn_ref, kp_ref, vp_ref, cpast_ref, cnew_ref, ccol_ref,
                            o_ref, *, scale, heads):
    a_coef = scale * _LOG2E
    inv_scale = 1.0 / scale
    ts = q_ref.shape[0]
    past = kp_ref.shape[0] // heads
    cc = ccol_ref[...]
    row = lax.broadcasted_iota(jnp.int32, (ts, ts), 0)
    col = lax.broadcasted_iota(jnp.int32, (ts, ts), 1)
    group = _pick_tile(heads, (4, 2, 1))
    kw = kp_ref[...].reshape(past, heads * LANES).astype(BF16)
    vw = vp_ref[...].reshape(past, heads * LANES).astype(BF16)
    for h0 in range(0, heads, group):
        hs = list(range(h0, h0 + group))
        sls = {h: slice(h * LANES, (h + 1) * LANES) for h in hs}
        kp = {h: kw[:, sls[h]] for h in hs}
        vp = {h: vw[:, sls[h]] for h in hs}
        up = {h: _dot_nt(q_ref[:, sls[h]], kp[h]) - cpast_ref[h] * inv_scale for h in hs}
        un = {h: jnp.where(col <= row, _dot_nt(q_ref[:, sls[h]], kn_ref[:, sls[h]]) - cnew_ref[h] * inv_scale,
                           NEG_INF) for h in hs}
        r = {}
        for h in hs:
            ci = _head_column(cc, h) * inv_scale
            m = jnp.maximum(jnp.max(up[h], axis=1, keepdims=True), jnp.max(un[h], axis=1, keepdims=True)) + ci
            r[h] = m - ci
        for h in hs:
            pp = jnp.exp2(a_coef * (up[h] - r[h]))
            pn = jnp.exp2(a_coef * (un[h] - r[h]))
            l = jnp.sum(pp, axis=1, keepdims=True) + jnp.sum(pn, axis=1, keepdims=True)
            o = (_dot(pp.astype(BF16), vp[h]) + _dot(pn.astype(BF16), vn_ref[:, sls[h]])) / l
            o_ref[:, sls[h]] = o.astype(o_ref.dtype)


def _fox_attn_sample(q, kb, vb, k_past, v_past, layer, c_past, c_new, c_col, batch, ts, past, heads):
    m, d = q.shape
    c_past = c_past.reshape(batch, heads, 1, past)
    c_new = c_new.reshape(batch, heads, 1, ts)
    return pl.pallas_call(
        functools.partial(_fox_attn_sample_kernel, scale=LANES ** -0.5, heads=heads),
        grid=(batch,),
        in_specs=[
            pl.BlockSpec((ts, d), lambda b: (b, 0)),
            pl.BlockSpec((ts, d), lambda b: (b, 0)),
            pl.BlockSpec((ts, d), lambda b: (b, 0)),
            pl.BlockSpec((None, past * heads, LANES), lambda b: (layer, b, 0)),
            pl.BlockSpec((None, past * heads, LANES), lambda b: (layer, b, 0)),
            pl.BlockSpec((None, heads, 1, past), lambda b: (b, 0, 0, 0)),
            pl.BlockSpec((None, heads, 1, ts), lambda b: (b, 0, 0, 0)),
            pl.BlockSpec((ts, heads), lambda b: (b, 0)),
        ],
        out_specs=pl.BlockSpec((ts, d), lambda b: (b, 0)),
        out_shape=jax.ShapeDtypeStruct((m, d), BF16),
        compiler_params=_cparams("parallel"),
        name="fox_attn_sample",
    )(q, kb, vb, k_past, v_past, c_past, c_new, c_col)


def _expand_rows(f, reps):
    parts = [jnp.broadcast_to(f[r:r + 1, :], (reps, f.shape[1])) for r in range(f.shape[0])]
    return parts[0] if len(parts) == 1 else jnp.concatenate(parts, axis=0)


def _take_rows(f, idx):
    parts = [f[r:r + 1, :] for r in idx]
    return parts[0] if len(parts) == 1 else jnp.concatenate(parts, axis=0)


def _hgrn_group(q, z, v, go, states, bscr_ref, lb, ng, kscale, n):
    n_g = q.shape[0] // n
    hp = q.shape[1] // LANES
    nblk = n // HGRN_SUB
    tiles = [(u, hh) for u in range(n_g) for hh in range(hp)]

    def tile(x, u, hh):
        return x[u * n:(u + 1) * n, hh * LANES:(hh + 1) * LANES]

    e = jnp.exp(-jnp.abs(z))
    r1 = 1.0 / (1.0 + e)
    er1 = e * r1
    pos = z > 0.0
    g = jnp.log(lb + (1.0 - lb) * jnp.where(pos, r1, er1))
    kk = (1.0 - lb) * jnp.where(pos, er1, r1)
    qf = q * _sigmoid(q) * kscale
    gate = go * _sigmoid(go)
    vb = v.astype(BF16)

    row = lax.broadcasted_iota(jnp.int32, (n, n), 0)
    col = lax.broadcasted_iota(jnp.int32, (n, n), 1)
    tri = jnp.where(col <= row, 1.0, 0.0).astype(BF16)
    g_hi = g.astype(BF16)
    g_r1 = g - g_hi.astype(F32)
    g_mid = g_r1.astype(BF16)
    g_lo = (g_r1 - g_mid.astype(F32)).astype(BF16)
    b = {t: _dot(tri, tile(g_hi, *t)) + _dot(tri, tile(g_mid, *t)) + _dot(tri, tile(g_lo, *t))
         for t in tiles}
    for (u, hh) in tiles:
        bscr_ref[hh, u] = b[u, hh]
    bmid = {(u, hh): bscr_ref[hh, u, pl.ds(HGRN_SUB // 2, nblk, stride=HGRN_SUB), :] for (u, hh) in tiles}
    bend = {(u, hh): bscr_ref[hh, u, pl.ds(HGRN_SUB - 1, nblk, stride=HGRN_SUB), :] for (u, hh) in tiles}
    blast = {t: bend[t][nblk - 1:nblk, :] for t in tiles}
    qd, kd = {}, {}
    for t in tiles:
        bmid_rows = _expand_rows(bmid[t], HGRN_SUB)
        dplus = jnp.exp(b[t] - bmid_rows)
        qd[t] = tile(qf, *t) * dplus
        kd[t] = tile(kk, *t) * (1.0 / dplus)

    blk = lax.broadcasted_iota(jnp.int32, (nblk, LANES), 0)
    a = {t: None for t in tiles}
    s = nblk // 2
    while s >= 1:
        upper = ((blk >> (s.bit_length() - 1)) & 1) == 1
        group_shift = (2 * s * HGRN_SUB).bit_length() - 1
        same_group = (row >> group_shift) == (col >> group_shift)
        for t in tiles:
            ref = _take_rows(bend[t], [(i // (2 * s)) * 2 * s + s - 1 for i in range(nblk)])
            fq = jnp.where(upper, jnp.exp(jnp.minimum(bmid[t] - ref, 0.0)), 0.0)
            fk = jnp.where(upper, 0.0, jnp.exp(jnp.minimum(ref - bmid[t], 0.0)))
            ql = qd[t] * _expand_rows(fq, HGRN_SUB)
            kl = kd[t] * _expand_rows(fk, HGRN_SUB)
            prod = _dot_nt(ql.astype(BF16), kl.astype(BF16))
            a[t] = prod if a[t] is None else jnp.where(same_group, prod, a[t])
        s //= 2
    sub_shift = HGRN_SUB.bit_length() - 1
    on_diag = ((row >> sub_shift) == (col >> sub_shift)) & (col <= row)
    o_in, q_in, upd, decay = {}, {}, {}, {}
    for t in tiles:
        pd = _dot_nt(qd[t].astype(BF16), kd[t].astype(BF16))
        at = jnp.where(on_diag, pd, 0.0 if a[t] is None else a[t])
        vt = tile(vb, *t)
        o_in[t] = _dot(at.astype(BF16), vt)
        q_in[t] = (qd[t] * _expand_rows(jnp.exp(bmid[t]), HGRN_SUB)).astype(BF16)
        k_out = kd[t] * _expand_rows(jnp.exp(blast[t] - bmid[t]), HGRN_SUB)
        upd[t] = _dot_tn(vt, k_out.astype(BF16))
        decay[t] = jnp.exp(blast[t])

    states = list(states)
    rows_out = []
    for u in range(n_g):
        cols_out = []
        for hh in range(hp):
            t = (u, hh)
            o = o_in[t] + _dot_nt(q_in[t], states[hh].astype(BF16))
            states[hh] = states[hh] * decay[t] + upd[t]
            o = o * lax.rsqrt(jnp.mean(o * o, axis=-1, keepdims=True) + RMS_EPS) * ng
            cols_out.append(o * tile(gate, *t))
        rows_out.append(cols_out[0] if hp == 1 else jnp.concatenate(cols_out, axis=1))
    out = rows_out[0] if n_g == 1 else jnp.concatenate(rows_out, axis=0)
    return out, states


def _hgrn_kernel(*refs, chunk, n_chunks, has_s0, layer, kscale, hp):
    if has_s0:
        q_ref, z_ref, i_ref, g_ref, lbp_ref, ng_ref, s0_ref, o_ref, s_ref, bscr_ref = refs
    else:
        q_ref, z_ref, i_ref, g_ref, lbp_ref, ng_ref, o_ref, s_ref, bscr_ref = refs
        s0_ref = None

    lbp = lbp_ref[...]
    e = jnp.exp(lbp - jnp.max(lbp, axis=0, keepdims=True))
    soft = e / jnp.sum(e, axis=0, keepdims=True)
    r = lax.broadcasted_iota(jnp.int32, lbp.shape, 0)
    lb_all = jnp.sum(jnp.where((r >= 1) & (r <= layer), soft, 0.0), axis=0, keepdims=True)
    ng = ng_ref[...]

    group = bscr_ref.shape[1]

    def step(t, states):
        rows = pl.ds(pl.multiple_of(t * group * chunk, group * chunk), group * chunk)
        o, states = _hgrn_group(q_ref[rows, :], z_ref[rows, :], i_ref[rows, :], g_ref[rows, :],
                                states, bscr_ref, lb_all, ng, kscale, chunk)
        o_ref[rows, :] = o.astype(o_ref.dtype)
        return tuple(states)

    if has_s0:
        init = tuple(s0_ref[hh].T for hh in range(hp))
    else:
        init = tuple(jnp.zeros((LANES, LANES), F32) for _ in range(hp))
    if n_chunks == group:
        states = step(0, init)
    else:
        states = lax.fori_loop(0, n_chunks // group, step, init)
    for hh in range(hp):
        s_ref[hh] = states[hh].T


def _hgrn(proj, lb_param, norm_g, s0, batch, seq, heads, layer, mixer_idx):
    m = proj.shape[0]
    d = heads * LANES
    chunk = _pick_tile(seq, (HGRN_CHUNK, 64, 32, 16))
    assert chunk % HGRN_SUB == 0 and seq % chunk == 0
    depth = lb_param.shape[0]
    has_s0 = s0 is not None
    hp = _pick_tile(heads, (2, 1) if seq > chunk else (4, 2, 1))
    group = _pick_tile(seq // chunk, (4, 2, 1))
    ng = heads // hp
    w = hp * LANES
    in_specs = [
        pl.BlockSpec((seq, w), lambda b, g: (b, g)),
        pl.BlockSpec((seq, w), lambda b, g: (b, ng + g)),
        pl.BlockSpec((seq, w), lambda b, g: (b, 2 * ng + g)),
        pl.BlockSpec((seq, w), lambda b, g: (b, 3 * ng + g)),
        pl.BlockSpec((depth, w), lambda b, g: (0, g)),
        pl.BlockSpec((None, 1, LANES), lambda b, g: (mixer_idx, 0, 0)),
    ]
    args = [proj, proj, proj, proj, lb_param, norm_g]
    if has_s0:
        in_specs.append(pl.BlockSpec((None, hp, LANES, LANES), lambda b, g: (b, g, 0, 0)))
        args.append(s0)
    return pl.pallas_call(
        functools.partial(_hgrn_kernel, chunk=chunk, n_chunks=seq // chunk, has_s0=has_s0,
                          layer=layer, kscale=LANES ** -0.5, hp=hp),
        grid=(batch, ng),
        in_specs=in_specs,
        out_specs=[
            pl.BlockSpec((seq, w), lambda b, g: (b, g)),
            pl.BlockSpec((None, hp, LANES, LANES), lambda b, g: (b, g, 0, 0)),
        ],
        out_shape=[
            jax.ShapeDtypeStruct((m, d), BF16),
            jax.ShapeDtypeStruct((batch, heads, LANES, LANES), F32),
        ],
        scratch_shapes=[pltpu.VMEM((hp, group, chunk, LANES), F32)],
        compiler_params=_cparams("parallel", "parallel"),
        name="hgrn",
    )(*args)


def _outproj_ln_kernel(o_ref, w_ref, x_ref, g_ref, b_ref, y_ref, *, alpha, rows):
    for r0 in range(0, o_ref.shape[0], rows):
        sl = slice(r0, r0 + rows)
        r = alpha * x_ref[sl, :] + _dot(o_ref[sl, :], w_ref[...])
        y_ref[sl, :] = _layer_norm(r, g_ref[...], b_ref[...])


def _outproj_ln(o, w, x, ln_g, ln_b, layer, ln_idx, alpha):
    m, d = x.shape
    tm = _pick_tile(m, (512, 256, 128, 64, 32, 16))
    return pl.pallas_call(
        functools.partial(_outproj_ln_kernel, alpha=alpha, rows=_pick_tile(tm, (128,))),
        grid=(m // tm,),
        in_specs=[
            pl.BlockSpec((tm, d), lambda i: (i, 0)),
            pl.BlockSpec((None, d, d), lambda i: (layer, 0, 0)),
            pl.BlockSpec((tm, d), lambda i: (i, 0)),
            pl.BlockSpec((None, 1, d), lambda i: (ln_idx, 0, 0)),
            pl.BlockSpec((None, 1, d), lambda i: (ln_idx, 0, 0)),
        ],
        out_specs=pl.BlockSpec((tm, d), lambda i: (i, 0)),
        out_shape=jax.ShapeDtypeStruct((m, d), F32),
        compiler_params=_cparams("parallel"),
        name="outproj_ln",
    )(o, w, x, ln_g, ln_b)


def _cast_pad_cols_kernel(x_ref, o_ref):
    f = x_ref.shape[1]
    o_ref[:, :f] = x_ref[...].astype(o_ref.dtype)
    if o_ref.shape[1] > f:
        o_ref[:, f:] = jnp.zeros((o_ref.shape[0], o_ref.shape[1] - f), o_ref.dtype)


def _cast_pad_cols(w, half, f, fp):
    n, d, _ = w.shape
    rb = _pick_tile(d, (256, 128, 64, 32, 16))
    return pl.pallas_call(
        _cast_pad_cols_kernel,
        grid=(n, d // rb),
        in_specs=[pl.BlockSpec((None, rb, f), lambda l, r: (l, r, half))],
        out_specs=pl.BlockSpec((None, rb, fp), lambda l, r: (l, r, 0)),
        out_shape=jax.ShapeDtypeStruct((n, d, fp), BF16),
        compiler_params=_cparams("parallel", "parallel"),
        name="cast_pad_cols",
    )(w)


def _cast_rows_kernel(x_ref, *rest, row0, valid):
    o_ref = rest[-1]
    rb = x_ref.shape[0]
    row = row0 + pl.program_id(1) * rb + lax.broadcasted_iota(jnp.int32, x_ref.shape, 0)
    o_ref[...] = jnp.where(row < valid, x_ref[...], 0.0).astype(o_ref.dtype)


def _cast_pad_rows(w, fp, tf):
    n, f, d = w.shape
    f_main = f // tf * tf
    rb = _pick_tile(f_main, (1280, 1024, 640, 512, 256, 128, 64, 32, 16))
    out = pl.pallas_call(
        functools.partial(_cast_rows_kernel, row0=0, valid=f),
        grid=(n, f_main // rb),
        in_specs=[pl.BlockSpec((None, rb, d), lambda l, r: (l, r, 0))],
        out_specs=pl.BlockSpec((None, rb, d), lambda l, r: (l, r, 0)),
        out_shape=jax.ShapeDtypeStruct((n, fp, d), BF16),
        compiler_params=_cparams("parallel", "parallel"),
        name="cast_rows",
    )(w)
    if fp == f_main:
        return out
    rb = math.gcd(math.gcd(f_main, f), fp)
    assert rb % 16 == 0
    first, last = f_main // rb, f // rb - 1
    return pl.pallas_call(
        functools.partial(_cast_rows_kernel, row0=f_main, valid=f),
        grid=(n, (fp - f_main) // rb),
        in_specs=[pl.BlockSpec((None, rb, d), lambda l, r: (l, jnp.minimum(first + r, last), 0)),
                  pl.BlockSpec(memory_space=pl.ANY)],
        out_specs=pl.BlockSpec((None, rb, d), lambda l, r: (l, first + r, 0)),
        out_shape=jax.ShapeDtypeStruct((n, fp, d), BF16),
        input_output_aliases={1: 0},
        compiler_params=_cparams("parallel", "parallel"),
        name="cast_rows_tail",
    )(w, out)


def _prep_ffn(w_up, w_down, tf):
    f = w_down.shape[1]
    fp = _round_up(f, tf)
    assert f % LANES == 0 and f >= tf
    return _cast_pad_cols(w_up, 0, f, fp), _cast_pad_cols(w_up, 1, f, fp), _cast_pad_rows(w_down, fp, tf)


def _cumsum_heads(logf_bsh):
    b, s, h = logf_bsh.shape
    sp = _round_up(s, LANES)
    x = jnp.transpose(logf_bsh, (0, 2, 1)).reshape(b * h, s)
    x = jnp.pad(x, ((0, 0), (0, sp - s)))
    return _cumsum_lanes(x)[:, :s]


def kernel(x_prompt, x_sample, cache_fox_k, cache_fox_v, cache_fox_logf, state_hgrn, ln_g, ln_b, ffn1_up, ffn1_down, ffn2_up, ffn2_down, fox_w_in, fox_b_f, fox_w_out, hgrn_w_in, hgrn_lb, hgrn_norm_g, hgrn_w_out):
    bp, tp, d = x_prompt.shape
    bs, ts, _ = x_sample.shape
    depth = ln_g.shape[0]
    heads = fox_b_f.shape[1]
    n_fox = fox_w_in.shape[0]
    past = cache_fox_k.shape[2]
    assert d == heads * LANES and hgrn_norm_g.shape[1] == LANES
    alpha = (2 * depth) ** 0.25

    tf = 512 if ffn1_down.shape[1] >= 512 else LANES
    ffn1 = _prep_ffn(ffn1_up, ffn1_down, tf)
    ffn2 = _prep_ffn(ffn2_up, ffn2_down, tf)
    fox_qkv = fox_w_in
    fox_wf = jnp.pad(fox_w_in[..., 3 * d:], ((0, 0), (0, 0), (0, LANES - heads))).astype(BF16)
    fox_bf = jnp.pad(fox_b_f, ((0, 0), (0, LANES - heads)))[:, None, :]
    fox_out = fox_w_out.astype(BF16)
    hgrn_out = hgrn_w_out.astype(BF16)
    hgrn_ng = hgrn_norm_g[:, None, :]
    g3 = ln_g.reshape(depth * 3, 1, d)
    b3 = ln_b.reshape(depth * 3, 1, d)

    xp = x_prompt.reshape(bp * tp, d)
    xs = x_sample.reshape(bs * ts, d)
    kv_p, kv_s = None, None
    logf_p, logf_s, hg_p, hg_s = [], [], [], []

    for i in range(depth):
        j = i // 2
        xp, xpb = _ffn_ln(xp, ffn1, g3, b3, i, 3 * i, alpha, tf, with_bf16=True)
        xs, xsb = _ffn_ln(xs, ffn1, g3, b3, i, 3 * i, alpha, tf, with_bf16=True)
        if i % 2 == 0:
            q, kb, vb, k_st, v_st, lf = _fox_proj(xpb, fox_qkv, fox_wf, fox_bf, j, n_fox, heads, kv_p)
            kv_p = (k_st, v_st)
            logf = lf[:, :heads].reshape(bp, tp, heads)
            c_row = _cumsum_heads(logf)
            c_col = jnp.transpose(c_row.reshape(bp, heads, tp), (0, 2, 1)).reshape(bp * tp, heads)
            op = _fox_attn(q, kb, vb, c_row, c_col, bp, tp, heads)
            logf_p.append(logf)
            q, kb, vb, k_st, v_st, lf = _fox_proj(xsb, fox_qkv, fox_wf, fox_bf, j, n_fox, heads, kv_s)
            kv_s = (k_st, v_st)
            logf = lf[:, :heads].reshape(bs, ts, heads)
            c_all = _cumsum_heads(jnp.concatenate([cache_fox_logf[j].astype(F32), logf], axis=1))
            c_new = c_all[:, past:]
            c_col = jnp.transpose(c_new.reshape(bs, heads, ts), (0, 2, 1)).reshape(bs * ts, heads)
            os_ = _fox_attn_sample(q, kb, vb, cache_fox_k.reshape(n_fox, bs * past * heads, LANES),
                                   cache_fox_v.reshape(n_fox, bs * past * heads, LANES), j,
                                   c_all[:, :past], c_new, c_col, bs, ts, past, heads)
            logf_s.append(logf)
            w_out = fox_out
        else:
            pp = _proj(xpb, hgrn_w_in, j, 4 * d)
            op, sp = _hgrn(pp, hgrn_lb, hgrn_ng, None, bp, tp, heads, i, j)
            ps = _proj(xsb, hgrn_w_in, j, 4 * d)
            os_, ss = _hgrn(ps, hgrn_lb, hgrn_ng, state_hgrn[j], bs, ts, heads, i, j)
            hg_p.append(sp)
            hg_s.append(ss)
            w_out = hgrn_out
        xp = _outproj_ln(op, w_out, xp, g3, b3, j, 3 * i + 1, alpha)
        xs = _outproj_ln(os_, w_out, xs, g3, b3, j, 3 * i + 1, alpha)
        xp = _ffn_ln(xp, ffn2, g3, b3, i, 3 * i + 2, alpha, tf)
        xs = _ffn_ln(xs, ffn2, g3, b3, i, 3 * i + 2, alpha, tf)

    return (xp.reshape(bp, tp, d), xs.reshape(bs, ts, d),
            kv_p[0].reshape(n_fox, bp, tp, heads, LANES), kv_p[1].reshape(n_fox, bp, tp, heads, LANES),
            jnp.stack(logf_p), jnp.stack(hg_p),
            kv_s[0].reshape(n_fox, bs, ts, heads, LANES), kv_s[1].reshape(n_fox, bs, ts, heads, LANES),
            jnp.stack(logf_s), jnp.stack(hg_s))
```

```python
import functools
import math

import jax
import jax.numpy as jnp
from jax import lax
from jax.experimental import pallas as pl
from jax.experimental.pallas import tpu as pltpu

F32 = jnp.float32
BF16 = jnp.bfloat16

LN_EPS = 1e-5
RMS_EPS = 1e-6
NEG_INF = -1e30

LANES = 128
HGRN_SUB = 16
HGRN_CHUNK = 128
VMEM_LIMIT_BYTES = 48 * 1024 * 1024

def _round_up(n, m):
    return (n + m - 1) // m * m


def _pick_tile(n, candidates):
    for c in candidates:
        if n % c == 0:
            return c
    return n


def _cparams(*semantics):
    return pltpu.CompilerParams(dimension_semantics=semantics,
                                vmem_limit_bytes=VMEM_LIMIT_BYTES)


def _dot(a, b):
    return jnp.dot(a, b, preferred_element_type=F32)


def _dot_nt(a, b):
    return lax.dot_general(a, b, (((1,), (1,)), ((), ())), preferred_element_type=F32)


def _dot_tn(a, b):
    return lax.dot_general(a, b, (((0,), (0,)), ((), ())), preferred_element_type=F32)


def _sigmoid(x):
    return 1.0 / (1.0 + jnp.exp(-x))


def _log_sigmoid(x):
    return jnp.minimum(x, 0.0) - jnp.log(1.0 + jnp.exp(-jnp.abs(x)))


def _layer_norm(r, g, b):
    mu = jnp.mean(r, axis=-1, keepdims=True)
    c = r - mu
    var = jnp.mean(c * c, axis=-1, keepdims=True)
    return c * lax.rsqrt(var + LN_EPS) * g + b


def _ffn_ln_kernel(x_ref, wg_ref, wu_ref, wd_ref, g_ref, b_ref, *rest, alpha, n_tiles, ln_rows):
    out_refs, (xb_ref, acc_ref) = rest[:-2], rest[-2:]
    i = pl.program_id(0)
    f = pl.program_id(1)
    slot = lax.rem(i, 2)
    tm = xb_ref.shape[0]

    def norm_rows(src_slot):
        start = pl.multiple_of(jnp.minimum(f * ln_rows, tm - ln_rows), 16)
        rows = pl.ds(start, ln_rows)
        y = _layer_norm(0.5 * acc_ref[src_slot, rows, :], g_ref[...], b_ref[...])
        for o_ref in out_refs:
            o_ref[rows, :] = y.astype(o_ref.dtype)

    @pl.when((i == 0) & (f == 0))
    def _():
        acc_ref[1] = jnp.zeros(acc_ref.shape[1:], F32)

    @pl.when(i < n_tiles)
    def _():
        @pl.when(f == 0)
        def _():
            x = x_ref[...]
            xb_ref[...] = x.astype(BF16)
            acc_ref[slot] = (2.0 * alpha) * x

        norm_rows(1 - slot)
        xb = xb_ref[...]
        gate = _dot(xb, wg_ref[...])
        up = _dot(xb, wu_ref[...])
        h = gate * _sigmoid(gate) * up
        acc_ref[slot] += _dot(h.astype(BF16), wd_ref[...])

    @pl.when(i == n_tiles)
    def _():
        norm_rows(1 - slot)


def _ffn_ln(x, weights, ln_g, ln_b, layer, ln_idx, alpha, tf, with_bf16=False):
    w_gate, w_up, w_down = weights
    m, d = x.shape
    fp = w_down.shape[1]
    nf = fp // tf
    tm = _pick_tile(m, (512, 256, 128, 64, 32, 16))
    nt = m // tm
    ln_rows = min(tm, _round_up(-(-tm // nf), 16))
    out_dtypes = (F32, BF16) if with_bf16 else (F32,)

    def w_step(i, f):
        return jnp.where(i < nt, f, nf - 1)

    outs = pl.pallas_call(
        functools.partial(_ffn_ln_kernel, alpha=alpha, n_tiles=nt, ln_rows=ln_rows),
        grid=(nt + 1, nf),
        in_specs=[
            pl.BlockSpec((tm, d), lambda i, f: (jnp.minimum(i, nt - 1), 0)),
            pl.BlockSpec((None, d, tf), lambda i, f: (layer, 0, w_step(i, f))),
            pl.BlockSpec((None, d, tf), lambda i, f: (layer, 0, w_step(i, f))),
            pl.BlockSpec((None, tf, d), lambda i, f: (layer, w_step(i, f), 0)),
            pl.BlockSpec((None, 1, d), lambda i, f: (ln_idx, 0, 0)),
            pl.BlockSpec((None, 1, d), lambda i, f: (ln_idx, 0, 0)),
        ],
        out_specs=[pl.BlockSpec((tm, d), lambda i, f: (jnp.maximum(i - 1, 0), 0)) for _ in out_dtypes],
        out_shape=[jax.ShapeDtypeStruct((m, d), dt) for dt in out_dtypes],
        scratch_shapes=[pltpu.VMEM((tm, d), BF16), pltpu.VMEM((2, tm, d), F32)],
        compiler_params=_cparams("arbitrary", "arbitrary"),
        name="ffn_ln",
    )(x, w_gate, w_up, w_down, ln_g, ln_b)
    return outs if with_bf16 else outs[0]


def _proj_kernel(x_ref, w_ref, o_ref):
    o_ref[...] = _dot(x_ref[...], w_ref[...]).astype(o_ref.dtype)


def _proj(xb, w, layer, n, out_dtype=F32):
    m, d = xb.shape
    tm = _pick_tile(m, (2048, 1024, 512, 256, 128, 64, 32, 16))
    tn = _pick_tile(n, (512, 256, 128))
    return pl.pallas_call(
        _proj_kernel,
        grid=(m // tm, n // tn),
        in_specs=[
            pl.BlockSpec((tm, d), lambda i, j: (i, 0)),
            pl.BlockSpec((None, d, tn), lambda i, j: (layer, 0, j)),
        ],
        out_specs=pl.BlockSpec((tm, tn), lambda i, j: (i, j)),
        out_shape=jax.ShapeDtypeStruct((m, n), out_dtype),
        compiler_params=_cparams("parallel", "arbitrary"),
        name="proj",
    )(xb, w)


def _fox_proj_kernel(x_ref, w_ref, wf_ref, bf_ref, *rest, nq, heads, layer):
    q_ref, kb_ref, vb_ref, k_ref, v_ref, lf_ref = rest[-6:]
    c = pl.program_id(0)
    tm, tn = q_ref.shape

    def cache_rows(o_ref, y):
        y3 = y.reshape(tm, tn // LANES, LANES)
        if len(o_ref.shape) == 3:
            o_ref[...] = y3
        else:
            for s in range(o_ref.shape[0]):
                o_ref[s] = y3 if s == layer else jnp.zeros_like(y3)

    @pl.when(c < nq)
    def _():
        q_ref[...] = _dot(x_ref[...], w_ref[...]).astype(q_ref.dtype)

    @pl.when((c >= nq) & (c < 2 * nq))
    def _():
        y = _dot(x_ref[...], w_ref[...])
        kb_ref[...] = y.astype(kb_ref.dtype)
        cache_rows(k_ref, y)

    @pl.when((c >= 2 * nq) & (c < 3 * nq))
    def _():
        y = _dot(x_ref[...], w_ref[...])
        vb_ref[...] = y.astype(vb_ref.dtype)
        cache_rows(v_ref, y)

    @pl.when(c == 3 * nq)
    def _():
        lf_ref[...] = _log_sigmoid(_dot(x_ref[...], wf_ref[...]) + bf_ref[...])


def _fox_proj(xb, w_qkv, wf, bf, layer, n_layers, heads, kv_stacks):
    m, d = xb.shape
    tm = _pick_tile(m, (512, 256, 128, 64, 32, 16))
    tn = 1024 if d % 1024 == 0 else d
    nq = d // tn
    hpt = tn // LANES
    assert hpt % 8 == 0 or hpt == heads
    nt = m // tm

    def rows(c, i, lo):
        return jnp.where(c < lo, 0, jnp.where(c < lo + nq, i, nt - 1))

    def cols(c, lo):
        return jnp.clip(c - lo, 0, nq - 1)

    in_specs = [
        pl.BlockSpec((tm, d), lambda c, i: (i, 0)),
        pl.BlockSpec((None, d, tn), lambda c, i: (layer, 0, jnp.minimum(c, 3 * nq - 1))),
        pl.BlockSpec((None, d, LANES), lambda c, i: (layer, 0, 0)),
        pl.BlockSpec((None, 1, LANES), lambda c, i: (layer, 0, 0)),
    ]
    args = [xb, w_qkv, wf, bf]
    aliases = {}
    if kv_stacks is not None:
        in_specs += [pl.BlockSpec(memory_space=pl.ANY), pl.BlockSpec(memory_space=pl.ANY)]
        args += list(kv_stacks)
        aliases = {4: 3, 5: 4}
        slab, slab0 = None, layer
    else:
        slab, slab0 = n_layers, 0
    return pl.pallas_call(
        functools.partial(_fox_proj_kernel, nq=nq, heads=heads, layer=layer),
        grid=(3 * nq + 1, nt),
        in_specs=in_specs,
        out_specs=[
            pl.BlockSpec((tm, tn), lambda c, i: (rows(c, i, 0), cols(c, 0))),
            pl.BlockSpec((tm, tn), lambda c, i: (rows(c, i, nq), cols(c, nq))),
            pl.BlockSpec((tm, tn), lambda c, i: (rows(c, i, 2 * nq), cols(c, 2 * nq))),
            pl.BlockSpec((slab, tm, hpt, LANES), lambda c, i: (slab0, rows(c, i, nq), cols(c, nq), 0)),
            pl.BlockSpec((slab, tm, hpt, LANES), lambda c, i: (slab0, rows(c, i, 2 * nq), cols(c, 2 * nq), 0)),
            pl.BlockSpec((tm, LANES), lambda c, i: (jnp.where(c == 3 * nq, i, 0), 0)),
        ],
        out_shape=[
            jax.ShapeDtypeStruct((m, d), BF16),
            jax.ShapeDtypeStruct((m, d), BF16),
            jax.ShapeDtypeStruct((m, d), BF16),
            jax.ShapeDtypeStruct((n_layers, m, heads, LANES), F32),
            jax.ShapeDtypeStruct((n_layers, m, heads, LANES), F32),
            jax.ShapeDtypeStruct((m, LANES), F32),
        ],
        input_output_aliases=aliases,
        compiler_params=_cparams("arbitrary", "arbitrary"),
        name="fox_proj",
    )(*args)


def _cumsum_lanes_kernel(x_ref, o_ref):
    x = x_ref[...]
    n = x.shape[-1]
    lane = lax.broadcasted_iota(jnp.int32, x.shape, 1)
    s = 1
    while s < n:
        x = x + jnp.where(lane >= s, pltpu.roll(x, s, 1), 0.0)
        s *= 2
    o_ref[...] = x


def _cumsum_lanes(x):
    r, n = x.shape
    rb = _pick_tile(r, (64, 32, 16, 8))
    return pl.pallas_call(
        _cumsum_lanes_kernel,
        grid=(r // rb,),
        in_specs=[pl.BlockSpec((rb, n), lambda i: (i, 0))],
        out_specs=pl.BlockSpec((rb, n), lambda i: (i, 0)),
        out_shape=jax.ShapeDtypeStruct((r, n), F32),
        compiler_params=_cparams("parallel"),
        name="cumsum_lanes",
    )(x)


def _head_column(c, h):
    lane = lax.broadcasted_iota(jnp.int32, c.shape, 1)
    return jnp.sum(jnp.where(lane == h, c, 0.0), axis=1, keepdims=True)


_LOG2E = math.log2(math.e)


def _fox_attn_kernel(q_ref, kb_ref, vb_ref, crow_ref, ccol_ref, o_ref, vext_ref, *, tq, scale, hp):
    g = pl.program_id(1)
    a_coef = scale * _LOG2E
    inv_scale = 1.0 / scale
    seq = q_ref.shape[0]
    for hh in range(hp):
        vext_ref[:, 2 * hh * LANES:(2 * hh + 1) * LANES] = vb_ref[:, hh * LANES:(hh + 1) * LANES]
        vext_ref[:, (2 * hh + 1) * LANES:2 * (hh + 1) * LANES] = jnp.ones((seq, LANES), BF16)
    cc = ccol_ref[...]
    row = lax.broadcasted_iota(jnp.int32, (tq, tq), 0)
    col = lax.broadcasted_iota(jnp.int32, (tq, tq), 1)
    ci_all = [_head_column(cc, g * hp + hh) * inv_scale for hh in range(hp)]
    cj_all = [crow_ref[hh] * inv_scale for hh in range(hp)]
    starts = list(range(0, seq, tq))
    pairs = [starts[i:i + 1] + starts[len(starts) - 1 - i:len(starts) - i]
             for i in range(len(starts) // 2)] or [starts]
    for pair in pairs:
        items = [(hh, r0) for r0 in pair for hh in range(hp)]
        sls = {hh: slice(hh * LANES, (hh + 1) * LANES) for hh in range(hp)}
        u = {}
        for (hh, r0) in items:
            w = r0 + tq
            u[hh, r0] = _dot_nt(q_ref[r0:w, sls[hh]], kb_ref[0:w, sls[hh]]) - cj_all[hh][:, 0:w]
        r = {}
        for (hh, r0) in items:
            w = r0 + tq
            diag = jnp.where(col <= row, u[hh, r0][:, r0:w], NEG_INF)
            u[hh, r0] = diag if r0 == 0 else jnp.concatenate([u[hh, r0][:, 0:r0], diag], axis=1)
            ci = ci_all[hh][r0:w, :]
            m = jnp.max(u[hh, r0], axis=1, keepdims=True) + ci
            r[hh, r0] = m - ci
        p = {t: jnp.exp2(a_coef * (u[t] - r[t])).astype(BF16) for t in items}
        for (hh, r0) in items:
            w = r0 + tq
            ov = _dot(p[hh, r0], vext_ref[0:w, 2 * hh * LANES:2 * (hh + 1) * LANES])
            o_ref[r0:w, sls[hh]] = (ov[:, :LANES] / ov[:, LANES:]).astype(o_ref.dtype)


def _fox_attn(q, kb, vb, c_row, c_col, batch, seq, heads):
    m, d = q.shape
    hp = 2 if heads % 2 == 0 else 1
    tq = _pick_tile(seq, (256, 128))
    ng = heads // hp
    w = hp * LANES
    c_row = c_row.reshape(batch * ng, hp, 1, seq)
    return pl.pallas_call(
        functools.partial(_fox_attn_kernel, tq=tq, scale=LANES ** -0.5, hp=hp),
        grid=(batch, ng),
        in_specs=[
            pl.BlockSpec((seq, w), lambda b, g: (b, g)),
            pl.BlockSpec((seq, w), lambda b, g: (b, g)),
            pl.BlockSpec((seq, w), lambda b, g: (b, g)),
            pl.BlockSpec((None, hp, 1, seq), lambda b, g: (b * ng + g, 0, 0, 0)),
            pl.BlockSpec((seq, heads), lambda b, g: (b, 0)),
        ],
        out_specs=pl.BlockSpec((seq, w), lambda b, g: (b, g)),
        out_shape=jax.ShapeDtypeStruct((m, d), BF16),
        scratch_shapes=[pltpu.VMEM((seq, 2 * w), BF16)],
        compiler_params=_cparams("parallel", "parallel"),
        name="fox_attn",
    )(q, kb, vb, c_row, c_col)


def _fox_attn_sample_kernel(q_ref, kn_ref, vn_ref, kp_ref, vp_ref, cpast_ref, cnew_ref, ccol_ref,
                            o_ref, *, scale, heads):
    a_coef = scale * _LOG2E
    inv_scale = 1.0 / scale
    ts = q_ref.shape[0]
    past = kp_ref.shape[0] // heads
    cc = ccol_ref[...]
    row = lax.broadcasted_iota(jnp.int32, (ts, ts), 0)
    col = lax.broadcasted_iota(jnp.int32, (ts, ts), 1)
    group = _pick_tile(heads, (4, 2, 1))
    kw = kp_ref[...].reshape(past, heads * LANES).astype(BF16)
    vw = vp_ref[...].reshape(past, heads * LANES).astype(BF16)
    for h0 in range(0, heads, group):
        hs = list(range(h0, h0 + group))
        sls = {h: slice(h * LANES, (h + 1) * LANES) for h in hs}
        kp = {h: kw[:, sls[h]] for h in hs}
        vp = {h: vw[:, sls[h]] for h in hs}
        up = {h: _dot_nt(q_ref[:, sls[h]], kp[h]) - cpast_ref[h] * inv_scale for h in hs}
        un = {h: jnp.where(col <= row, _dot_nt(q_ref[:, sls[h]], kn_ref[:, sls[h]]) - cnew_ref[h] * inv_scale,
                           NEG_INF) for h in hs}
        r = {}
        for h in hs:
            ci = _head_column(cc, h) * inv_scale
            m = jnp.maximum(jnp.max(up[h], axis=1, keepdims=True), jnp.max(un[h], axis=1, keepdims=True)) + ci
            r[h] = m - ci
        for h in hs:
            pp = jnp.exp2(a_coef * (up[h] - r[h]))
            pn = jnp.exp2(a_coef * (un[h] - r[h]))
            l = jnp.sum(pp, axis=1, keepdims=True) + jnp.sum(pn, axis=1, keepdims=True)
            o = (_dot(pp.astype(BF16), vp[h]) + _dot(pn.astype(BF16), vn_ref[:, sls[h]])) / l
            o_ref[:, sls[h]] = o.astype(o_ref.dtype)


def _fox_attn_sample(q, kb, vb, k_past, v_past, layer, c_past, c_new, c_col, batch, ts, past, heads):
    m, d = q.shape
    c_past = c_past.reshape(batch, heads, 1, past)
    c_new = c_new.reshape(batch, heads, 1, ts)
    return pl.pallas_call(
        functools.partial(_fox_attn_sample_kernel, scale=LANES ** -0.5, heads=heads),
        grid=(batch,),
        in_specs=[
            pl.BlockSpec((ts, d), lambda b: (b, 0)),
            pl.BlockSpec((ts, d), lambda b: (b, 0)),
            pl.BlockSpec((ts, d), lambda b: (b, 0)),
            pl.BlockSpec((None, past * heads, LANES), lambda b: (layer, b, 0)),
            pl.BlockSpec((None, past * heads, LANES), lambda b: (layer, b, 0)),
            pl.BlockSpec((None, heads, 1, past), lambda b: (b, 0, 0, 0)),
            pl.BlockSpec((None, heads, 1, ts), lambda b: (b, 0, 0, 0)),
            pl.BlockSpec((ts, heads), lambda b: (b, 0)),
        ],
        out_specs=pl.BlockSpec((ts, d), lambda b: (b, 0)),
        out_shape=jax.ShapeDtypeStruct((m, d), BF16),
        compiler_params=_cparams("parallel"),
        name="fox_attn_sample",
    )(q, kb, vb, k_past, v_past, c_past, c_new, c_col)


def _expand_rows(f, reps):
    parts = [jnp.broadcast_to(f[r:r + 1, :], (reps, f.shape[1])) for r in range(f.shape[0])]
    return parts[0] if len(parts) == 1 else jnp.concatenate(parts, axis=0)


def _take_rows(f, idx):
    parts = [f[r:r + 1, :] for r in idx]
    return parts[0] if len(parts) == 1 else jnp.concatenate(parts, axis=0)


def _hgrn_group(q, z, v, go, states, bscr_ref, lb, ng, kscale, n):
    n_g = q.shape[0] // n
    hp = q.shape[1] // LANES
    nblk = n // HGRN_SUB
    tiles = [(u, hh) for u in range(n_g) for hh in range(hp)]

    def tile(x, u, hh):
        return x[u * n:(u + 1) * n, hh * LANES:(hh + 1) * LANES]

    e = jnp.exp(-jnp.abs(z))
    r1 = 1.0 / (1.0 + e)
    er1 = e * r1
    pos = z > 0.0
    g = jnp.log(lb + (1.0 - lb) * jnp.where(pos, r1, er1))
    kk = (1.0 - lb) * jnp.where(pos, er1, r1)
    qf = q * _sigmoid(q) * kscale
    gate = go * _sigmoid(go)
    vb = v.astype(BF16)

    row = lax.broadcasted_iota(jnp.int32, (n, n), 0)
    col = lax.broadcasted_iota(jnp.int32, (n, n), 1)
    tri = jnp.where(col <= row, 1.0, 0.0).astype(BF16)
    g_hi = g.astype(BF16)
    g_r1 = g - g_hi.astype(F32)
    g_mid = g_r1.astype(BF16)
    g_lo = (g_r1 - g_mid.astype(F32)).astype(BF16)
    b = {t: _dot(tri, tile(g_hi, *t)) + _dot(tri, tile(g_mid, *t)) + _dot(tri, tile(g_lo, *t))
         for t in tiles}
    for (u, hh) in tiles:
        bscr_ref[hh, u] = b[u, hh]
    bmid = {(u, hh): bscr_ref[hh, u, pl.ds(HGRN_SUB // 2, nblk, stride=HGRN_SUB), :] for (u, hh) in tiles}
    bend = {(u, hh): bscr_ref[hh, u, pl.ds(HGRN_SUB - 1, nblk, stride=HGRN_SUB), :] for (u, hh) in tiles}
    blast = {t: bend[t][nblk - 1:nblk, :] for t in tiles}
    qd, kd = {}, {}
    for t in tiles:
        bmid_rows = _expand_rows(bmid[t], HGRN_SUB)
        qd[t] = tile(qf, *t) * jnp.exp(b[t] - bmid_rows)
        kd[t] = tile(kk, *t) * jnp.exp(bmid_rows - b[t])

    blk = lax.broadcasted_iota(jnp.int32, (nblk, LANES), 0)
    a = {t: None for t in tiles}
    s = nblk // 2
    while s >= 1:
        upper = ((blk >> (s.bit_length() - 1)) & 1) == 1
        group_shift = (2 * s * HGRN_SUB).bit_length() - 1
        same_group = (row >> group_shift) == (col >> group_shift)
        for t in tiles:
            ref = _take_rows(bend[t], [(i // (2 * s)) * 2 * s + s - 1 for i in range(nblk)])
            fq = jnp.where(upper, jnp.exp(jnp.minimum(bmid[t] - ref, 0.0)), 0.0)
            fk = jnp.where(upper, 0.0, jnp.exp(jnp.minimum(ref - bmid[t], 0.0)))
            ql = qd[t] * _expand_rows(fq, HGRN_SUB)
            kl = kd[t] * _expand_rows(fk, HGRN_SUB)
            prod = _dot_nt(ql.astype(BF16), kl.astype(BF16))
            a[t] = prod if a[t] is None else jnp.where(same_group, prod, a[t])
        s //= 2
    sub_shift = HGRN_SUB.bit_length() - 1
    on_diag = ((row >> sub_shift) == (col >> sub_shift)) & (col <= row)
    o_in, q_in, upd, decay = {}, {}, {}, {}
    for t in tiles:
        pd = _dot_nt(qd[t].astype(BF16), kd[t].astype(BF16))
        at = jnp.where(on_diag, pd, 0.0 if a[t] is None else a[t])
        vt = tile(vb, *t)
        o_in[t] = _dot(at.astype(BF16), vt)
        q_in[t] = (qd[t] * _expand_rows(jnp.exp(bmid[t]), HGRN_SUB)).astype(BF16)
        k_out = kd[t] * _expand_rows(jnp.exp(blast[t] - bmid[t]), HGRN_SUB)
        upd[t] = _dot_tn(vt, k_out.astype(BF16))
        decay[t] = jnp.exp(blast[t])

    states = list(states)
    rows_out = []
    for u in range(n_g):
        cols_out = []
        for hh in range(hp):
            t = (u, hh)
            o = o_in[t] + _dot_nt(q_in[t], states[hh].astype(BF16))
            states[hh] = states[hh] * decay[t] + upd[t]
            o = o * lax.rsqrt(jnp.mean(o * o, axis=-1, keepdims=True) + RMS_EPS) * ng
            cols_out.append(o * tile(gate, *t))
        rows_out.append(cols_out[0] if hp == 1 else jnp.concatenate(cols_out, axis=1))
    out = rows_out[0] if n_g == 1 else jnp.concatenate(rows_out, axis=0)
    return out, states


def _hgrn_kernel(*refs, chunk, n_chunks, has_s0, layer, kscale, hp):
    if has_s0:
        q_ref, z_ref, i_ref, g_ref, lbp_ref, ng_ref, s0_ref, o_ref, s_ref, bscr_ref = refs
    else:
        q_ref, z_ref, i_ref, g_ref, lbp_ref, ng_ref, o_ref, s_ref, bscr_ref = refs
        s0_ref = None

    lbp = lbp_ref[...]
    e = jnp.exp(lbp - jnp.max(lbp, axis=0, keepdims=True))
    soft = e / jnp.sum(e, axis=0, keepdims=True)
    r = lax.broadcasted_iota(jnp.int32, lbp.shape, 0)
    lb_all = jnp.sum(jnp.where((r >= 1) & (r <= layer), soft, 0.0), axis=0, keepdims=True)
    ng = ng_ref[...]

    group = bscr_ref.shape[1]

    def step(t, states):
        rows = pl.ds(pl.multiple_of(t * group * chunk, group * chunk), group * chunk)
        o, states = _hgrn_group(q_ref[rows, :], z_ref[rows, :], i_ref[rows, :], g_ref[rows, :],
                                states, bscr_ref, lb_all, ng, kscale, chunk)
        o_ref[rows, :] = o.astype(o_ref.dtype)
        return tuple(states)

    if has_s0:
        init = tuple(s0_ref[hh].T for hh in range(hp))
    else:
        init = tuple(jnp.zeros((LANES, LANES), F32) for _ in range(hp))
    if n_chunks == group:
        states = step(0, init)
    else:
        states = lax.fori_loop(0, n_chunks // group, step, init)
    for hh in range(hp):
        s_ref[hh] = states[hh].T


def _hgrn(proj, lb_param, norm_g, s0, batch, seq, heads, layer, mixer_idx):
    m = proj.shape[0]
    d = heads * LANES
    chunk = _pick_tile(seq, (HGRN_CHUNK, 64, 32, 16))
    assert chunk % HGRN_SUB == 0 and seq % chunk == 0
    depth = lb_param.shape[0]
    has_s0 = s0 is not None
    hp = _pick_tile(heads, (2, 1) if seq > chunk else (4, 2, 1))
    group = _pick_tile(seq // chunk, (4, 2, 1))
    ng = heads // hp
    w = hp * LANES
    in_specs = [
        pl.BlockSpec((seq, w), lambda b, g: (b, g)),
        pl.BlockSpec((seq, w), lambda b, g: (b, ng + g)),
        pl.BlockSpec((seq, w), lambda b, g: (b, 2 * ng + g)),
        pl.BlockSpec((seq, w), lambda b, g: (b, 3 * ng + g)),
        pl.BlockSpec((depth, w), lambda b, g: (0, g)),
        pl.BlockSpec((None, 1, LANES), lambda b, g: (mixer_idx, 0, 0)),
    ]
    args = [proj, proj, proj, proj, lb_param, norm_g]
    if has_s0:
        in_specs.append(pl.BlockSpec((None, hp, LANES, LANES), lambda b, g: (b, g, 0, 0)))
        args.append(s0)
    return pl.pallas_call(
        functools.partial(_hgrn_kernel, chunk=chunk, n_chunks=seq // chunk, has_s0=has_s0,
                          layer=layer, kscale=LANES ** -0.5, hp=hp),
        grid=(batch, ng),
        in_specs=in_specs,
        out_specs=[
            pl.BlockSpec((seq, w), lambda b, g: (b, g)),
            pl.BlockSpec((None, hp, LANES, LANES), lambda b, g: (b, g, 0, 0)),
        ],
        out_shape=[
            jax.ShapeDtypeStruct((m, d), BF16),
            jax.ShapeDtypeStruct((batch, heads, LANES, LANES), F32),
        ],
        scratch_shapes=[pltpu.VMEM((hp, group, chunk, LANES), F32)],
        compiler_params=_cparams("parallel", "parallel"),
        name="hgrn",
    )(*args)


def _outproj_ln_kernel(o_ref, w_ref, x_ref, g_ref, b_ref, y_ref, *, alpha, rows):
    for r0 in range(0, o_ref.shape[0], rows):
        sl = slice(r0, r0 + rows)
        r = alpha * x_ref[sl, :] + _dot(o_ref[sl, :], w_ref[...])
        y_ref[sl, :] = _layer_norm(r, g_ref[...], b_ref[...])


def _outproj_ln(o, w, x, ln_g, ln_b, layer, ln_idx, alpha):
    m, d = x.shape
    tm = _pick_tile(m, (512, 256, 128, 64, 32, 16))
    return pl.pallas_call(
        functools.partial(_outproj_ln_kernel, alpha=alpha, rows=_pick_tile(tm, (128,))),
        grid=(m // tm,),
        in_specs=[
            pl.BlockSpec((tm, d), lambda i: (i, 0)),
            pl.BlockSpec((None, d, d), lambda i: (layer, 0, 0)),
            pl.BlockSpec((tm, d), lambda i: (i, 0)),
            pl.BlockSpec((None, 1, d), lambda i: (ln_idx, 0, 0)),
            pl.BlockSpec((None, 1, d), lambda i: (ln_idx, 0, 0)),
        ],
        out_specs=pl.BlockSpec((tm, d), lambda i: (i, 0)),
        out_shape=jax.ShapeDtypeStruct((m, d), F32),
        compiler_params=_cparams("parallel"),
        name="outproj_ln",
    )(o, w, x, ln_g, ln_b)


def _cast_pad_cols_kernel(x_ref, o_ref):
    f = x_ref.shape[1]
    o_ref[:, :f] = x_ref[...].astype(o_ref.dtype)
    if o_ref.shape[1] > f:
        o_ref[:, f:] = jnp.zeros((o_ref.shape[0], o_ref.shape[1] - f), o_ref.dtype)


def _cast_pad_cols(w, half, f, fp):
    n, d, _ = w.shape
    rb = _pick_tile(d, (256, 128, 64, 32, 16))
    return pl.pallas_call(
        _cast_pad_cols_kernel,
        grid=(n, d // rb),
        in_specs=[pl.BlockSpec((None, rb, f), lambda l, r: (l, r, half))],
        out_specs=pl.BlockSpec((None, rb, fp), lambda l, r: (l, r, 0)),
        out_shape=jax.ShapeDtypeStruct((n, d, fp), BF16),
        compiler_params=_cparams("parallel", "parallel"),
        name="cast_pad_cols",
    )(w)


def _cast_rows_kernel(x_ref, *rest, row0, valid):
    o_ref = rest[-1]
    rb = x_ref.shape[0]
    row = row0 + pl.program_id(1) * rb + lax.broadcasted_iota(jnp.int32, x_ref.shape, 0)
    o_ref[...] = jnp.where(row < valid, x_ref[...], 0.0).astype(o_ref.dtype)


def _cast_pad_rows(w, fp, tf):
    n, f, d = w.shape
    f_main = f // tf * tf
    rb = tf
    last_main = f_main // rb - 1
    out = pl.pallas_call(
        functools.partial(_cast_rows_kernel, row0=0, valid=f_main),
        grid=(n, fp // rb),
        in_specs=[pl.BlockSpec((None, rb, d), lambda l, r: (l, jnp.minimum(r, last_main), 0))],
        out_specs=pl.BlockSpec((None, rb, d), lambda l, r: (l, r, 0)),
        out_shape=jax.ShapeDtypeStruct((n, fp, d), BF16),
        compiler_params=_cparams("parallel", "parallel"),
        name="cast_rows",
    )(w)
    if fp == f_main:
        return out
    rb = math.gcd(math.gcd(f_main, f), fp)
    assert rb % 16 == 0
    first, last = f_main // rb, f // rb - 1
    return pl.pallas_call(
        functools.partial(_cast_rows_kernel, row0=f_main, valid=f),
        grid=(n, (fp - f_main) // rb),
        in_specs=[pl.BlockSpec((None, rb, d), lambda l, r: (l, jnp.minimum(first + r, last), 0)),
                  pl.BlockSpec(memory_space=pl.ANY)],
        out_specs=pl.BlockSpec((None, rb, d), lambda l, r: (l, first + r, 0)),
        out_shape=jax.ShapeDtypeStruct((n, fp, d), BF16),
        input_output_aliases={1: 0},
        compiler_params=_cparams("parallel", "parallel"),
        name="cast_rows_tail",
    )(w, out)


def _prep_ffn(w_up, w_down, tf):
    f = w_down.shape[1]
    fp = _round_up(f, tf)
    assert f % LANES == 0 and f >= tf
    return _cast_pad_cols(w_up, 0, f, fp), _cast_pad_cols(w_up, 1, f, fp), _cast_pad_rows(w_down, fp, tf)


def _cumsum_heads(logf_bsh):
    b, s, h = logf_bsh.shape
    sp = _round_up(s, LANES)
    x = jnp.transpose(logf_bsh, (0, 2, 1)).reshape(b * h, s)
    x = jnp.pad(x, ((0, 0), (0, sp - s)))
    return _cumsum_lanes(x)[:, :s]


def kernel(x_prompt, x_sample, cache_fox_k, cache_fox_v, cache_fox_logf, state_hgrn, ln_g, ln_b, ffn1_up, ffn1_down, ffn2_up, ffn2_down, fox_w_in, fox_b_f, fox_w_out, hgrn_w_in, hgrn_lb, hgrn_norm_g, hgrn_w_out):
    bp, tp, d = x_prompt.shape
    bs, ts, _ = x_sample.shape
    depth = ln_g.shape[0]
    heads = fox_b_f.shape[1]
    n_fox = fox_w_in.shape[0]
    past = cache_fox_k.shape[2]
    assert d == heads * LANES and hgrn_norm_g.shape[1] == LANES
    alpha = (2 * depth) ** 0.25

    tf = 512 if ffn1_down.shape[1] >= 512 else LANES
    ffn1 = _prep_ffn(ffn1_up, ffn1_down, tf)
    ffn2 = _prep_ffn(ffn2_up, ffn2_down, tf)
    fox_qkv = fox_w_in[..., :3 * d].astype(BF16)
    fox_wf = jnp.pad(fox_w_in[..., 3 * d:], ((0, 0), (0, 0), (0, LANES - heads))).astype(BF16)
    fox_bf = jnp.pad(fox_b_f, ((0, 0), (0, LANES - heads)))[:, None, :]
    fox_out = fox_w_out.astype(BF16)
    hgrn_in = hgrn_w_in.astype(BF16)
    hgrn_out = hgrn_w_out.astype(BF16)
    hgrn_ng = hgrn_norm_g[:, None, :]
    g3 = ln_g.reshape(depth * 3, 1, d)
    b3 = ln_b.reshape(depth * 3, 1, d)

    xp = x_prompt.reshape(bp * tp, d)
    xs = x_sample.reshape(bs * ts, d)
    kv_p, kv_s = None, None
    logf_p, logf_s, hg_p, hg_s = [], [], [], []

    for i in range(depth):
        j = i // 2
        xp, xpb = _ffn_ln(xp, ffn1, g3, b3, i, 3 * i, alpha, tf, with_bf16=True)
        xs, xsb = _ffn_ln(xs, ffn1, g3, b3, i, 3 * i, alpha, tf, with_bf16=True)
        if i % 2 == 0:
            q, kb, vb, k_st, v_st, lf = _fox_proj(xpb, fox_qkv, fox_wf, fox_bf, j, n_fox, heads, kv_p)
            kv_p = (k_st, v_st)
            logf = lf[:, :heads].reshape(bp, tp, heads)
            c_row = _cumsum_heads(logf)
            c_col = jnp.transpose(c_row.reshape(bp, heads, tp), (0, 2, 1)).reshape(bp * tp, heads)
            op = _fox_attn(q, kb, vb, c_row, c_col, bp, tp, heads)
            logf_p.append(logf)
            q, kb, vb, k_st, v_st, lf = _fox_proj(xsb, fox_qkv, fox_wf, fox_bf, j, n_fox, heads, kv_s)
            kv_s = (k_st, v_st)
            logf = lf[:, :heads].reshape(bs, ts, heads)
            c_all = _cumsum_heads(jnp.concatenate([cache_fox_logf[j].astype(F32), logf], axis=1))
            c_new = c_all[:, past:]
            c_col = jnp.transpose(c_new.reshape(bs, heads, ts), (0, 2, 1)).reshape(bs * ts, heads)
            os_ = _fox_attn_sample(q, kb, vb, cache_fox_k.reshape(n_fox, bs * past * heads, LANES),
                                   cache_fox_v.reshape(n_fox, bs * past * heads, LANES), j,
                                   c_all[:, :past], c_new, c_col, bs, ts, past, heads)
            logf_s.append(logf)
            w_out = fox_out
        else:
            pp = _proj(xpb, hgrn_in, j, 4 * d)
            op, sp = _hgrn(pp, hgrn_lb, hgrn_ng, None, bp, tp, heads, i, j)
            ps = _proj(xsb, hgrn_in, j, 4 * d)
            os_, ss = _hgrn(ps, hgrn_lb, hgrn_ng, state_hgrn[j], bs, ts, heads, i, j)
            hg_p.append(sp)
            hg_s.append(ss)
            w_out = hgrn_out
        xp = _outproj_ln(op, w_out, xp, g3, b3, j, 3 * i + 1, alpha)
        xs = _outproj_ln(os_, w_out, xs, g3, b3, j, 3 * i + 1, alpha)
        xp = _ffn_ln(xp, ffn2, g3, b3, i, 3 * i + 2, alpha, tf)
        xs = _ffn_ln(xs, ffn2, g3, b3, i, 3 * i + 2, alpha, tf)

    return (xp.reshape(bp, tp, d), xs.reshape(bs, ts, d),
            kv_p[0].reshape(n_fox, bp, tp, heads, LANES), kv_p[1].reshape(n_fox, bp, tp, heads, LANES),
            jnp.stack(logf_p), jnp.stack(hg_p),
            kv_s[0].reshape(n_fox, bs, ts, heads, LANES), kv_s[1].reshape(n_fox, bs, ts, heads, LANES),
            jnp.stack(logf_s), jnp.stack(hg_s))
```
